```python
import math
import jax, jax.numpy as jnp
from jax import lax
import numpy as np

D_MODEL = 1024
BATCH = 32
SEQ = 2048
DEPTH = 1

MIX_WIDTH = D_MODEL
MOBA_WIDTH = MIX_WIDTH // 2
MOBA_HEAD_DIM = 64
MOBA_HEADS = MOBA_WIDTH // MOBA_HEAD_DIM
MOBA_BLOCK = 256
MOBA_TOPK = 3
QUERY_CHUNK = 128
GDN_WIDTH = MIX_WIDTH - MOBA_WIDTH
GDN_HEAD_DIM = 128
GDN_HEADS = GDN_WIDTH // GDN_HEAD_DIM
GDN_CONV = 4
GDN_CHUNK = 64
IN_SPLITS = (MOBA_WIDTH,) * 4 + (GDN_WIDTH,) * 4 + (GDN_HEADS, GDN_HEADS)
IN_WIDTH = sum(IN_SPLITS)
RMS_EPS = 1e-6
NEG_INF = -1e30

kernel_name = "hymba_moba_gated_deltanet_sandwich"


def rms_norm(x, w):
    xf = x.astype(jnp.float32)
    y = xf * lax.rsqrt(jnp.mean(xf * xf, axis=-1, keepdims=True) + RMS_EPS)
    return (y * w.astype(jnp.float32)).astype(x.dtype)


def alibi_slopes(n_heads):
    start = 2.0 ** (-8.0 / n_heads)
    return jnp.asarray(start ** np.arange(1, n_heads + 1), dtype=jnp.float32)


def moba_attention(q, k, v):
    B, T, H, dh = q.shape
    nb = -(-T // MOBA_BLOCK)
    pad = nb * MOBA_BLOCK - T
    n_qc = T // QUERY_CHUNK
    k_sel = min(MOBA_TOPK, nb - 1)
    scale = dh ** -0.5
    slopes = alibi_slopes(H)
    qh = q.transpose(0, 2, 1, 3)
    pad_cfg = ((0, 0), (0, pad), (0, 0), (0, 0))
    kb = jnp.pad(k, pad_cfg).reshape(B, nb, MOBA_BLOCK, H, dh).transpose(0, 3, 1, 2, 4)
    vb = jnp.pad(v, pad_cfg).reshape(B, nb, MOBA_BLOCK, H, dh).transpose(0, 3, 1, 2, 4)
    kbar = jnp.mean(kb.astype(jnp.float32), axis=3)
    blk_off = jnp.arange(MOBA_BLOCK)
    head_idx = jnp.arange(H)[:, None, None]

    def chunk_attend(step):
        bi = step // n_qc
        q0 = (step % n_qc) * QUERY_CHUNK
        own = q0 // MOBA_BLOCK
        qc = lax.dynamic_slice_in_dim(qh[bi], q0, QUERY_CHUNK, axis=1)
        kb_b, vb_b = kb[bi], vb[bi]
        t_pos = q0 + jnp.arange(QUERY_CHUNK)
        k_own = lax.dynamic_index_in_dim(kb_b, own, axis=1, keepdims=False)
        v_own = lax.dynamic_index_in_dim(vb_b, own, axis=1, keepdims=False)
        dist_own = (t_pos[:, None] - (own * MOBA_BLOCK + blk_off)[None, :]).astype(jnp.float32)
        logit_own = (jnp.einsum('hqd,hkd->hqk', qc, k_own).astype(jnp.float32) * scale
                     - slopes[:, None, None] * dist_own)
        logit_own = jnp.where(dist_own >= 0, logit_own, NEG_INF)
        if k_sel == 0:
            p = jax.nn.softmax(logit_own, axis=-1).astype(v.dtype)
            return jnp.einsum('hqk,hkd->hqd', p, v_own)
        gate = jnp.einsum('hqd,hnd->hqn', qc.astype(jnp.float32), kbar[bi])
        gate = jnp.where(jnp.arange(nb) < own, gate, NEG_INF)
        _, sel = lax.top_k(gate, k_sel)
        k_g = kb_b[head_idx, sel]
        v_g = vb_b[head_idx, sel]
        s_sel = sel[..., None] * MOBA_BLOCK + blk_off
        dist_sel = (t_pos[None, :, None, None] - s_sel).astype(jnp.float32)
        logit_sel = (jnp.einsum('hqd,hqnkd->hqnk', qc, k_g).astype(jnp.float32) * scale
                     - slopes[:, None, None, None] * dist_sel)
        valid = (jnp.arange(k_sel) < own)[None, None, :, None]
        logit_sel = jnp.where(valid, logit_sel, NEG_INF).reshape(H, QUERY_CHUNK, k_sel * MOBA_BLOCK)
        p = jax.nn.softmax(jnp.concatenate([logit_sel, logit_own], axis=-1), axis=-1).astype(v.dtype)
        p_sel = p[..., :k_sel * MOBA_BLOCK].reshape(H, QUERY_CHUNK, k_sel, MOBA_BLOCK)
        p_own = p[..., k_sel * MOBA_BLOCK:]
        return (jnp.einsum('hqnk,hqnkd->hqd', p_sel, v_g)
                + jnp.einsum('hqk,hkd->hqd', p_own, v_own))

    out = lax.map(chunk_attend, jnp.arange(B * n_qc))
    out = out.reshape(B, n_qc, H, QUERY_CHUNK, dh).transpose(0, 1, 3, 2, 4)
    return out.reshape(B, T, H * dh)


def causal_depthwise_conv(x, w):
    K, C = w.shape
    return lax.conv_general_dilated(
        x, w[:, None, :].astype(x.dtype), window_strides=(1,), padding=((K - 1, 0),),
        dimension_numbers=('NWC', 'WIO', 'NWC'), feature_group_count=C)


def l2_normalize(x):
    return x * lax.rsqrt(jnp.sum(x * x, axis=-1, keepdims=True) + RMS_EPS)


def gated_delta_rule(q, k, v, g, beta):
    B, T, H, dk = q.shape
    dv = v.shape[-1]
    C = GDN_CHUNK
    nc = T // C
    q = l2_normalize(q) * (dk ** -0.5)
    k = l2_normalize(k)

    def chunks(t):
        return t.reshape(B, nc, C, H, -1).transpose(1, 0, 3, 2, 4)

    q, k, v = chunks(q), chunks(k), chunks(v)
    beta = beta.reshape(B, nc, C, H).transpose(1, 0, 3, 2)
    gc = jnp.cumsum(g.reshape(B, nc, C, H).transpose(1, 0, 3, 2), axis=-1)
    causal = jnp.tril(jnp.ones((C, C), dtype=bool))
    strict = jnp.tril(jnp.ones((C, C), dtype=bool), -1)
    decay = jnp.exp(jnp.where(causal, gc[..., :, None] - gc[..., None, :], -jnp.inf))
    k_beta = k * beta[..., None]
    a = jnp.where(strict, jnp.einsum('nbhid,nbhjd->nbhij', k_beta, k) * decay, 0.0)
    eye = jnp.eye(C, dtype=jnp.float32)
    t_inv = lax.linalg.triangular_solve(a + eye, jnp.broadcast_to(eye, a.shape),
                                        left_side=True, lower=True, unit_diagonal=True)
    u = t_inv @ (v * beta[..., None])
    w = t_inv @ (k_beta * jnp.exp(gc)[..., None])
    intra = jnp.where(causal, jnp.einsum('nbhid,nbhjd->nbhij', q, k) * decay, 0.0)
    q_dec = q * jnp.exp(gc)[..., None]
    k_dec = k * jnp.exp(gc[..., -1:] - gc)[..., None]
    g_last = jnp.exp(gc[..., -1])

    def step(S, xs):
        q_i, k_i, u_i, w_i, intra_i, gl = xs
        v_new = u_i - w_i @ S
        o = q_i @ S + intra_i @ v_new
        S = S * gl[..., None, None] + jnp.einsum('bhcd,bhce->bhde', k_i, v_new)
        return S, o

    S0 = jnp.zeros((B, H, dk, dv), dtype=jnp.float32)
    _, o = lax.scan(step, S0, (q_dec, k_dec, u, w, intra, g_last))
    return o.transpose(1, 0, 3, 2, 4).reshape(B, T, H, dv)


def hybrid_layer(x, norm_pre_w, w_in, conv_w, a_log, dt_bias, gdn_norm_w, w_out, norm_post_w):
    B, T, _ = x.shape
    h = rms_norm(x, norm_pre_w)
    proj = h @ w_in.astype(x.dtype)
    split_at = list(np.cumsum(IN_SPLITS)[:-1])
    a_q, a_k, a_v, a_z, d_q, d_k, d_v, d_z, d_b, d_a = jnp.split(proj, split_at, axis=-1)

    heads_a = lambda t: t.reshape(B, T, MOBA_HEADS, MOBA_HEAD_DIM)
    moba_out = moba_attention(heads_a(a_q), heads_a(a_k), heads_a(a_v)) * jax.nn.silu(a_z)

    qkv = jax.nn.silu(causal_depthwise_conv(jnp.concatenate([d_q, d_k, d_v], axis=-1), conv_w))
    g_q, g_k, g_v = jnp.split(qkv, 3, axis=-1)
    heads_b = lambda t: t.reshape(B, T, GDN_HEADS, GDN_HEAD_DIM).astype(jnp.float32)
    beta = jax.nn.sigmoid(d_b.astype(jnp.float32))
    g = -jnp.exp(a_log.astype(jnp.float32)) * jax.nn.softplus(
        d_a.astype(jnp.float32) + dt_bias.astype(jnp.float32))
    o = gated_delta_rule(heads_b(g_q), heads_b(g_k), heads_b(g_v), g, beta)
    gdn_out = rms_norm(o, gdn_norm_w).reshape(B, T, GDN_WIDTH).astype(x.dtype) * jax.nn.silu(d_z)

    mixed = jnp.concatenate([moba_out, gdn_out], axis=-1) @ w_out.astype(x.dtype)
    return x + rms_norm(mixed, norm_post_w)


def setup_inputs(seed: int = 0) -> dict:
    key = jax.random.key(seed)
    ks = jax.random.split(key, 9)
    f32 = jnp.float32
    x = jax.random.normal(ks[0], (BATCH, SEQ, D_MODEL), f32)
    norm_pre_w = 1.0 + 0.05 * jax.random.normal(ks[1], (DEPTH, D_MODEL), f32)
    w_in = jax.random.normal(ks[2], (DEPTH, D_MODEL, IN_WIDTH), f32) * D_MODEL ** -0.5
    conv_w = jax.random.normal(ks[3], (DEPTH, GDN_CONV, 3 * GDN_WIDTH), f32) * GDN_CONV ** -0.5
    a_log = jnp.log(jax.random.uniform(ks[4], (DEPTH, GDN_HEADS), f32, minval=1.0, maxval=16.0))
    dt = jnp.exp(jax.random.uniform(ks[5], (DEPTH, GDN_HEADS), f32,
                                    minval=math.log(1e-3), maxval=math.log(1e-1)))
    dt_bias = dt + jnp.log(-jnp.expm1(-dt))
    gdn_norm_w = 1.0 + 0.05 * jax.random.normal(ks[6], (DEPTH, GDN_HEAD_DIM), f32)
    w_out = jax.random.normal(ks[7], (DEPTH, MIX_WIDTH, D_MODEL), f32) * MIX_WIDTH ** -0.5
    norm_post_w = 1.0 + 0.05 * jax.random.normal(ks[8], (DEPTH, D_MODEL), f32)
    return {"x": x, "norm_pre_w": norm_pre_w, "w_in": w_in, "conv_w": conv_w,
            "a_log": a_log, "dt_bias": dt_bias, "gdn_norm_w": gdn_norm_w,
            "w_out": w_out, "norm_post_w": norm_post_w}


def reference(x, norm_pre_w, w_in, conv_w, a_log, dt_bias, gdn_norm_w, w_out, norm_post_w):
    for layer in range(DEPTH):
        x = hybrid_layer(x, norm_pre_w[layer], w_in[layer], conv_w[layer], a_log[layer],
                         dt_bias[layer], gdn_norm_w[layer], w_out[layer], norm_post_w[layer])
    return x
```

```python
import functools

import numpy as np
import jax
import jax.numpy as jnp
from jax import lax
from jax.experimental import pallas as pl
from jax.experimental.pallas import tpu as pltpu

F32 = jnp.float32
BF16 = jnp.bfloat16

D_MODEL = 1024
MOBA_WIDTH = 512
MOBA_HEAD_DIM = 64
MOBA_HEADS = MOBA_WIDTH // MOBA_HEAD_DIM
MOBA_BLOCK = 256
MOBA_TOPK = 3
GDN_WIDTH = 512
GDN_HEAD_DIM = 128
GDN_HEADS = GDN_WIDTH // GDN_HEAD_DIM
GDN_CONV = 4
GDN_CHUNK = 64
GDN_SUPER = 256
RMS_EPS = 1e-6
NEG_INF = -1e30

LANES = 128
VMEM_LIMIT_BYTES = 48 * 1024 * 1024

NT_DIMS = (((1,), (1,)), ((), ()))
TN_DIMS = (((0,), (0,)), ((), ()))


def _dot(a, b):
    return jnp.dot(a.astype(BF16), b.astype(BF16), preferred_element_type=F32)


def _dot_nt(a, b):
    return lax.dot_general(a.astype(BF16), b.astype(BF16), NT_DIMS, preferred_element_type=F32)


def _silu(x):
    return x / (1.0 + jnp.exp(-x))


def _inproj_kernel(x_ref, nw_ref, wm_ref, wt_ref, moba_ref, gdn_ref, ba_ref):
    x = x_ref[...]
    h = x * lax.rsqrt(jnp.mean(x * x, axis=-1, keepdims=True) + RMS_EPS) * nw_ref[...]
    hb = h.astype(BF16)
    half = moba_ref.shape[-1]
    moba_ref[...] = jnp.dot(hb, wm_ref[:, :half], preferred_element_type=F32)
    gdn_ref[...] = jnp.dot(hb, wm_ref[:, half:], preferred_element_type=F32)
    ba_ref[...] = jnp.dot(hb, wt_ref[...], preferred_element_type=F32)


def _inproj(x2, norm_w, w_main, w_tail, tm):
    n = x2.shape[0]
    half = w_main.shape[1] // 2
    return pl.pallas_call(
        _inproj_kernel,
        grid=(n // tm,),
        in_specs=[
            pl.BlockSpec((tm, D_MODEL), lambda i: (i, 0)),
            pl.BlockSpec((1, D_MODEL), lambda i: (0, 0)),
            pl.BlockSpec(w_main.shape, lambda i: (0, 0)),
            pl.BlockSpec(w_tail.shape, lambda i: (0, 0)),
        ],
        out_specs=[
            pl.BlockSpec((tm, half), lambda i: (i, 0)),
            pl.BlockSpec((tm, half), lambda i: (i, 0)),
            pl.BlockSpec((tm, LANES), lambda i: (i, 0)),
        ],
        out_shape=[
            jax.ShapeDtypeStruct((n, half), F32),
            jax.ShapeDtypeStruct((n, half), F32),
            jax.ShapeDtypeStruct((n, LANES), F32),
        ],
        compiler_params=pltpu.CompilerParams(
            dimension_semantics=("arbitrary",), vmem_limit_bytes=VMEM_LIMIT_BYTES),
        name="inproj",
    )(x2, norm_w, w_main, w_tail)


def _moba_kernel(slopes_ref, q_ref, k_ref, v_ref, z_ref, o_ref, logit_scr):
    seq = q_ref.shape[1]
    blk = MOBA_BLOCK
    nb = seq // blk
    pair = pl.program_id(1)
    lane = lax.broadcasted_iota(jnp.int32, (1, LANES), 1)
    k_all = k_ref[0]
    kb16 = k_all.astype(BF16)
    vb16 = v_ref[0].astype(BF16)
    tile_row = lax.broadcasted_iota(jnp.int32, (LANES, LANES), 0)
    kbar = jnp.zeros((LANES, LANES), F32)
    for n in range(nb):
        kbar = jnp.where(tile_row == n, jnp.mean(k_all[n * blk:(n + 1) * blk], axis=0, keepdims=True), kbar)
    row = lax.broadcasted_iota(jnp.int32, (blk, blk), 0)
    col = lax.broadcasted_iota(jnp.int32, (blk, blk), 1)
    rel = (row - col).astype(F32)
    causal = row >= col

    for i in range(nb):
        rows = slice(i * blk, (i + 1) * blk)
        q_blk = q_ref[0, rows, :]
        outs = []
        for h in range(2):
            head_mask = (lane >= MOBA_HEAD_DIM * h) & (lane < MOBA_HEAD_DIM * (h + 1))
            slope = slopes_ref[pair * 2 + h]
            qh = jnp.where(head_mask, q_blk, 0.0)
            qs = (qh * (MOBA_HEAD_DIM ** -0.5)).astype(BF16)

            sel_bias = None
            if i > MOBA_TOPK:
                gate = lax.dot_general(qh, kbar, NT_DIMS, precision=lax.Precision.HIGHEST,
                                       preferred_element_type=F32)
                cnt = jnp.zeros((blk, LANES), F32)
                for s in range(1, i):
                    lower = pltpu.roll(gate, s, axis=1)
                    upper = pltpu.roll(gate, LANES - s, axis=1)
                    cnt = cnt + jnp.where((lower >= gate) & (lane >= s), 1.0, 0.0)
                    cnt = cnt + jnp.where((upper > gate) & (lane + s < i), 1.0, 0.0)
                sel_bias = jnp.where((cnt < MOBA_TOPK) & (lane < i), 0.0, NEG_INF)

            m = jnp.full((blk, 1), NEG_INF, F32)
            for n in range(i + 1):
                s_tile = lax.dot_general(qs, kb16[n * blk:(n + 1) * blk], NT_DIMS,
                                         preferred_element_type=F32)
                if n == i:
                    tile = jnp.where(causal, s_tile - slope * rel, NEG_INF)
                else:
                    tile = s_tile - slope * (rel + float((i - n) * blk))
                    if sel_bias is not None:
                        tile = tile + sel_bias[:, n:n + 1]
                logit_scr[n] = tile
                m = jnp.maximum(m, jnp.max(tile, axis=1, keepdims=True))
            l = jnp.zeros((blk, 1), F32)
            acc = jnp.zeros((blk, LANES), F32)
            for n in range(i + 1):
                p = jnp.exp(logit_scr[n] - m)
                l = l + jnp.sum(p, axis=1, keepdims=True)
                acc = acc + jnp.dot(p.astype(BF16), vb16[n * blk:(n + 1) * blk],
                                    preferred_element_type=F32)
            outs.append((acc / l, head_mask))
        out = jnp.where(outs[0][1], outs[0][0], outs[1][0])
        o_ref[0, rows, :] = out * _silu(z_ref[0, rows, :])


def _moba(slopes, proj_a, batch, seq):
    npair = MOBA_WIDTH // LANES
    proj_a = proj_a.reshape(batch, seq, 4 * MOBA_WIDTH)

    def col_block(base):
        return pl.BlockSpec((1, seq, LANES), lambda b, p, *_: (b, 0, base + p))

    return pl.pallas_call(
        _moba_kernel,
        grid_spec=pltpu.PrefetchScalarGridSpec(
            num_scalar_prefetch=1,
            grid=(batch, npair),
            in_specs=[col_block(0), col_block(npair), col_block(2 * npair), col_block(3 * npair)],
            out_specs=pl.BlockSpec((1, seq, LANES), lambda b, p, *_: (b, 0, p)),
            scratch_shapes=[pltpu.VMEM((seq // MOBA_BLOCK, MOBA_BLOCK, MOBA_BLOCK), F32)],
        ),
        out_shape=jax.ShapeDtypeStruct((batch, seq, MOBA_WIDTH), F32),
        compiler_params=pltpu.CompilerParams(
            dimension_semantics=("arbitrary", "arbitrary"), vmem_limit_bytes=VMEM_LIMIT_BYTES),
        name="moba",
    )(slopes, proj_a, proj_a, proj_a, proj_a)


def _unit_lower_inverse(a, row, col):
    eye = jnp.where(row == col, 1.0, 0.0)
    in16 = jnp.right_shift(row, 4) == jnp.right_shift(col, 4)
    in32 = jnp.right_shift(row, 5) == jnp.right_shift(col, 5)
    d = jnp.where(in16, a, 0.0)
    x = eye - d
    p = _dot(d, d)
    x = x + _dot(x, p)
    p = _dot(p, p)
    x = x + _dot(x, p)
    p = _dot(p, p)
    x = x + _dot(x, p)
    l32 = jnp.where(in32 & jnp.logical_not(in16), a, 0.0)
    x = x - _dot(x, _dot(l32, x))
    l64 = jnp.where(in32, 0.0, a)
    x = x - _dot(x, _dot(l64, x))
    return x


def _gdn_kernel(alog_ref, dtb_ref, xq_ref, xk_ref, xv_ref, xz_ref, cwq_ref, cwk_ref, cwv_ref,
                ba_ref, gnw_ref, o_ref,
                pad_scr, gate_scr, q_scr, k_scr, v_scr, qd_scr, kd_scr, u_scr, w_scr, egl_scr,
                intra_scr):
    seq = xq_ref.shape[1]
    sup = GDN_SUPER
    chunk = GDN_CHUNK
    per_sup = sup // chunk
    head = pl.program_id(1)
    lane = lax.broadcasted_iota(jnp.int32, (1, LANES), 1)
    row = lax.broadcasted_iota(jnp.int32, (sup, sup), 0)
    col = lax.broadcasted_iota(jnp.int32, (sup, sup), 1)
    chunk_shift = chunk.bit_length() - 1
    same_chunk = jnp.right_shift(row, chunk_shift) == jnp.right_shift(col, chunk_shift)
    causal = same_chunk & (row >= col)
    strict = same_chunk & (row > col)

    @pl.when(head == 0)
    def _():
        ba = ba_ref[0]
        alog = jnp.zeros((1, LANES), F32)
        dtb = jnp.zeros((1, LANES), F32)
        for hh in range(GDN_HEADS):
            alog = jnp.where(lane == GDN_HEADS + hh, alog_ref[hh], alog)
            dtb = jnp.where(lane == GDN_HEADS + hh, dtb_ref[hh], dtb)
        beta = 1.0 / (1.0 + jnp.exp(-ba))
        pre = ba + dtb
        softplus = jnp.maximum(pre, 0.0) + jnp.log(1.0 + jnp.exp(-jnp.abs(pre)))
        g = -jnp.exp(alog) * softplus
        tri = jnp.where(causal, 1.0, 0.0)
        for s in range(seq // sup):
            rows = slice(s * sup, (s + 1) * sup)
            gc = jnp.dot(tri, g[rows], precision=lax.Precision.HIGHEST, preferred_element_type=F32)
            gate_scr[rows, :] = jnp.where(lane < GDN_HEADS, beta[rows], gc)

    def conv_silu(x_ref, cw_ref):
        pad_scr[0:8, :] = jnp.zeros((8, LANES), F32)
        pad_scr[8:, :] = x_ref[0]
        acc = pad_scr[8:8 + seq, :] * cw_ref[GDN_CONV - 1:GDN_CONV, :]
        for j in range(GDN_CONV - 1):
            shift = GDN_CONV - 1 - j
            acc = acc + pad_scr[8 - shift:8 - shift + seq, :] * cw_ref[j:j + 1, :]
        return _silu(acc)

    q = conv_silu(xq_ref, cwq_ref)
    q_scr[...] = q * lax.rsqrt(jnp.sum(q * q, axis=-1, keepdims=True) + RMS_EPS) * (GDN_HEAD_DIM ** -0.5)
    k = conv_silu(xk_ref, cwk_ref)
    k_scr[...] = k * lax.rsqrt(jnp.sum(k * k, axis=-1, keepdims=True) + RMS_EPS)
    v_scr[...] = conv_silu(xv_ref, cwv_ref)

    def prepare(s, carry):
        r0 = pl.multiple_of(s * sup, sup)
        rows = pl.ds(r0, sup)
        gates = gate_scr[rows, :]
        beta = jnp.sum(jnp.where(lane == head, gates, 0.0), axis=1, keepdims=True)
        gc = jnp.sum(jnp.where(lane == GDN_HEADS + head, gates, 0.0), axis=1, keepdims=True)
        gc_b = jnp.broadcast_to(gc, (sup, LANES))
        gc_row = jnp.concatenate([gc_b[0:LANES].T, gc_b[LANES:].T], axis=1)[0:1, :]
        decay = jnp.exp(jnp.where(causal, gc - gc_row, NEG_INF))
        gl_b = jnp.concatenate(
            [jnp.broadcast_to(gc_b[(c + 1) * chunk - 1:(c + 1) * chunk, :], (chunk, LANES))
             for c in range(per_sup)], axis=0)
        egc = jnp.exp(gc)
        qb = q_scr[rows, :]
        kb = k_scr[rows, :]
        vb = v_scr[rows, :]
        k_beta = kb * beta
        a = jnp.where(strict, _dot_nt(k_beta, kb) * decay, 0.0)
        t_inv = _unit_lower_inverse(a, row, col)
        u_scr[rows, :] = _dot(t_inv, vb * beta)
        w_scr[rows, :] = _dot(t_inv, k_beta * egc)
        intra_scr[rows, :] = jnp.where(causal, _dot_nt(qb, kb) * decay, 0.0)
        qd_scr[rows, :] = qb * egc
        kd_scr[rows, :] = kb * jnp.exp(gl_b - gc_b)
        egl_scr[rows, :] = jnp.exp(gl_b)
        return carry

    lax.fori_loop(0, seq // sup, prepare, 0)

    gnw = gnw_ref[...]

    def scan(s, state):
        r0 = pl.multiple_of(s * sup, sup)
        v_parts = []
        for c in range(per_sup):
            rows = pl.ds(r0 + c * chunk, chunk)
            v_new = u_scr[rows, :] - _dot(w_scr[rows, :], state)
            v_parts.append(v_new)
            v_sup = jnp.concatenate(
                v_parts + [jnp.zeros(((per_sup - 1 - c) * chunk, LANES), F32)] * (c < per_sup - 1), axis=0)
            o = _dot(qd_scr[rows, :], state) + _dot(intra_scr[rows, :], v_sup)
            egl = egl_scr[rows, :]
            state = state * jnp.concatenate([egl, egl], axis=0) + lax.dot_general(
                kd_scr[rows, :].astype(BF16), v_new.astype(BF16), TN_DIMS, preferred_element_type=F32)
            y = o * lax.rsqrt(jnp.mean(o * o, axis=-1, keepdims=True) + RMS_EPS) * gnw
            o_ref[0, rows, :] = y * _silu(xz_ref[0, rows, :])
        return state

    lax.fori_loop(0, seq // sup, scan, jnp.zeros((GDN_HEAD_DIM, GDN_HEAD_DIM), F32))


def _gdn(a_log, dt_bias, proj_d, gates, conv_w, gdn_norm_w, batch, seq):
    nh = GDN_HEADS
    proj_d = proj_d.reshape(batch, seq, 4 * GDN_WIDTH)
    gates = gates.reshape(batch, seq, LANES)

    def col_block(base):
        return pl.BlockSpec((1, seq, LANES), lambda b, h, *_: (b, 0, base + h))

    def conv_block(base):
        return pl.BlockSpec((GDN_CONV, LANES), lambda b, h, *_: (0, base + h))

    tile = pltpu.VMEM((seq, LANES), F32)
    return pl.pallas_call(
        _gdn_kernel,
        grid_spec=pltpu.PrefetchScalarGridSpec(
            num_scalar_prefetch=2,
            grid=(batch, nh),
            in_specs=[col_block(0), col_block(nh), col_block(2 * nh), col_block(3 * nh),
                      conv_block(0), conv_block(nh), conv_block(2 * nh),
                      pl.BlockSpec((1, seq, LANES), lambda b, h, *_: (b, 0, 0)),
                      pl.BlockSpec((1, LANES), lambda b, h, *_: (0, 0))],
            out_specs=pl.BlockSpec((1, seq, LANES), lambda b, h, *_: (b, 0, h)),
            scratch_shapes=[pltpu.VMEM((seq + 8, LANES), F32)] + [tile] * 9
                           + [pltpu.VMEM((seq, GDN_SUPER), F32)],
        ),
        out_shape=jax.ShapeDtypeStruct((batch, seq, GDN_WIDTH), F32),
        compiler_params=pltpu.CompilerParams(
            dimension_semantics=("arbitrary", "arbitrary"), vmem_limit_bytes=VMEM_LIMIT_BYTES),
        name="gdn",
    )(a_log, dt_bias, proj_d, proj_d, proj_d, proj_d, conv_w, conv_w, conv_w, gates, gdn_norm_w)


def _outproj_kernel(x_ref, ma_ref, md_ref, wa_ref, wd_ref, nw_ref, o_ref):
    y = (jnp.dot(ma_ref[...].astype(BF16), wa_ref[...], preferred_element_type=F32)
         + jnp.dot(md_ref[...].astype(BF16), wd_ref[...], preferred_element_type=F32))
    y = y * lax.rsqrt(jnp.mean(y * y, axis=-1, keepdims=True) + RMS_EPS) * nw_ref[...]
    o_ref[...] = x_ref[...] + y


def _outproj(x2, mix_a, mix_d, w_a, w_d, norm_w, tm):
    n = x2.shape[0]
    return pl.pallas_call(
        _outproj_kernel,
        grid=(n // tm,),
        in_specs=[
            pl.BlockSpec((tm, D_MODEL), lambda i: (i, 0)),
            pl.BlockSpec((tm, MOBA_WIDTH), lambda i: (i, 0)),
            pl.BlockSpec((tm, GDN_WIDTH), lambda i: (i, 0)),
            pl.BlockSpec(w_a.shape, lambda i: (0, 0)),
            pl.BlockSpec(w_d.shape, lambda i: (0, 0)),
            pl.BlockSpec((1, D_MODEL), lambda i: (0, 0)),
        ],
        out_specs=pl.BlockSpec((tm, D_MODEL), lambda i: (i, 0)),
        out_shape=jax.ShapeDtypeStruct((n, D_MODEL), F32),
        compiler_params=pltpu.CompilerParams(
            dimension_semantics=("arbitrary",), vmem_limit_bytes=VMEM_LIMIT_BYTES),
        name="outproj",
    )(x2, mix_a, mix_d, w_a, w_d, norm_w)


def _alibi_slopes():
    start = 2.0 ** (-8.0 / MOBA_HEADS)
    slopes = np.asarray(start ** np.arange(1, MOBA_HEADS + 1), dtype=np.float32)
    return jnp.asarray(slopes)


def _layer(x, norm_pre_w, w_in, conv_w, a_log, dt_bias, gdn_norm_w, w_out, norm_post_w):
    batch, seq, _ = x.shape
    assert seq % MOBA_BLOCK == 0 and seq % GDN_SUPER == 0
    n = batch * seq
    tm = 512
    assert n % tm == 0
    x2 = x.reshape(n, D_MODEL)
    n_main = 4 * MOBA_WIDTH + 4 * GDN_WIDTH
    w_main = w_in[:, :n_main].astype(BF16)
    w_tail = jnp.pad(w_in[:, n_main:], ((0, 0), (0, LANES - 2 * GDN_HEADS))).astype(BF16)
    proj_a, proj_d, gates = _inproj(x2, norm_pre_w.reshape(1, D_MODEL), w_main, w_tail, tm)
    mix_a = _moba(_alibi_slopes(), proj_a, batch, seq)
    mix_d = _gdn(a_log.astype(F32), dt_bias.astype(F32), proj_d, gates, conv_w,
                 gdn_norm_w.reshape(1, GDN_HEAD_DIM), batch, seq)
    w_o = w_out.astype(BF16)
    out = _outproj(x2, mix_a.reshape(n, MOBA_WIDTH), mix_d.reshape(n, GDN_WIDTH),
                   w_o[:MOBA_WIDTH], w_o[MOBA_WIDTH:], norm_post_w.reshape(1, D_MODEL), tm)
    return out.reshape(batch, seq, D_MODEL)


def kernel(x, norm_pre_w, w_in, conv_w, a_log, dt_bias, gdn_norm_w, w_out, norm_post_w):
    for layer in range(norm_pre_w.shape[0]):
        x = _layer(x, norm_pre_w[layer], w_in[layer], conv_w[layer], a_log[layer], dt_bias[layer],
                   gdn_norm_w[layer], w_out[layer], norm_post_w[layer])
    return x
```

```python
import functools

import numpy as np
import jax
import jax.numpy as jnp
from jax import lax
from jax.experimental import pallas as pl
from jax.experimental.pallas import tpu as pltpu

F32 = jnp.float32
BF16 = jnp.bfloat16

D_MODEL = 1024
MOBA_WIDTH = 512
MOBA_HEAD_DIM = 64
MOBA_HEADS = MOBA_WIDTH // MOBA_HEAD_DIM
MOBA_BLOCK = 256
MOBA_TOPK = 3
GDN_WIDTH = 512
GDN_HEAD_DIM = 128
GDN_HEADS = GDN_WIDTH // GDN_HEAD_DIM
GDN_CONV = 4
GDN_CHUNK = 64
GDN_SUPER = 256
GDN_PREP_GROUP = 4
RMS_EPS = 1e-6
NEG_INF = -1e30

LANES = 128
VMEM_LIMIT_BYTES = 48 * 1024 * 1024

NT_DIMS = (((1,), (1,)), ((), ()))
TN_DIMS = (((0,), (0,)), ((), ()))


def _dot(a, b):
    return jnp.dot(a.astype(BF16), b.astype(BF16), preferred_element_type=F32)


def _dot_nt(a, b):
    return lax.dot_general(a.astype(BF16), b.astype(BF16), NT_DIMS, preferred_element_type=F32)


def _silu(x):
    return x / (1.0 + jnp.exp(-x))


def _inproj_kernel(x_ref, nw_ref, wm_ref, wt_ref, moba_ref, gdn_ref, ba_ref):
    x = x_ref[...]
    h = x * lax.rsqrt(jnp.mean(x * x, axis=-1, keepdims=True) + RMS_EPS) * nw_ref[...]
    hb = h.astype(BF16)
    half = moba_ref.shape[-1]
    moba_ref[...] = jnp.dot(hb, wm_ref[:, :half], preferred_element_type=F32)
    gdn_ref[...] = jnp.dot(hb, wm_ref[:, half:], preferred_element_type=F32)
    ba_ref[...] = jnp.dot(hb, wt_ref[...], preferred_element_type=F32)


def _inproj(x2, norm_w, w_main, w_tail, tm):
    n = x2.shape[0]
    half = w_main.shape[1] // 2
    return pl.pallas_call(
        _inproj_kernel,
        grid=(n // tm,),
        in_specs=[
            pl.BlockSpec((tm, D_MODEL), lambda i: (i, 0)),
            pl.BlockSpec((1, D_MODEL), lambda i: (0, 0)),
            pl.BlockSpec(w_main.shape, lambda i: (0, 0)),
            pl.BlockSpec(w_tail.shape, lambda i: (0, 0)),
        ],
        out_specs=[
            pl.BlockSpec((tm, half), lambda i: (i, 0)),
            pl.BlockSpec((tm, half), lambda i: (i, 0)),
            pl.BlockSpec((tm, LANES), lambda i: (i, 0)),
        ],
        out_shape=[
            jax.ShapeDtypeStruct((n, half), F32),
            jax.ShapeDtypeStruct((n, half), F32),
            jax.ShapeDtypeStruct((n, LANES), F32),
        ],
        compiler_params=pltpu.CompilerParams(
            dimension_semantics=("arbitrary",), vmem_limit_bytes=VMEM_LIMIT_BYTES),
        name="inproj",
    )(x2, norm_w, w_main, w_tail)


FEAT_POS_HI = MOBA_HEADS
FEAT_POS_LO = MOBA_HEADS + 1
FEAT_ONE_HI = MOBA_HEADS + 2
FEAT_ONE_LO = MOBA_HEADS + 3


def _moba_features(seq):
    blk = MOBA_BLOCK
    pos = np.arange(seq)
    kfeat = np.zeros((2, seq, LANES), np.float32)
    qfeat = np.zeros((2, 2, blk, LANES), np.float32)
    for h in range(2):
        base = MOBA_HEAD_DIM * (1 - h)
        kfeat[h, pos, base + pos // blk] = 1.0
        kfeat[h, :, base + FEAT_POS_HI] = blk * (pos // blk)
        kfeat[h, :, base + FEAT_POS_LO] = pos % blk
        kfeat[h, :, base + FEAT_ONE_HI] = 1.0
        kfeat[h, :, base + FEAT_ONE_LO] = 1.0
        qfeat[h, 0, :, base + FEAT_POS_HI] = 1.0
        qfeat[h, 0, :, base + FEAT_POS_LO] = 1.0
        qfeat[h, 0, :, base + FEAT_ONE_LO] = -np.arange(blk)
        qfeat[h, 1, :, base + FEAT_ONE_HI] = -blk
    return jnp.asarray(kfeat), jnp.asarray(qfeat)


def _moba_kernel(slopes_ref, q_ref, k_ref, v_ref, z_ref, kfeat_ref, qfeat_ref, o_ref,
                 kaug_scr, logit_scr):
    seq = q_ref.shape[1]
    blk = MOBA_BLOCK
    nb = seq // blk
    pair = pl.program_id(1)
    lane = lax.broadcasted_iota(jnp.int32, (1, LANES), 1)
    head_masks = [(lane >= MOBA_HEAD_DIM * h) & (lane < MOBA_HEAD_DIM * (h + 1)) for h in range(2)]
    feat_lane = [lane - MOBA_HEAD_DIM * (1 - h) for h in range(2)]
    slopes = [slopes_ref[pair * 2 + h] for h in range(2)]

    k_all = k_ref[0]
    vb16 = v_ref[0].astype(BF16)
    tile_row = lax.broadcasted_iota(jnp.int32, (LANES, LANES), 0)
    kbar = [jnp.zeros((LANES, LANES), F32) for _ in range(2)]
    for n in range(nb):
        mean_n = jnp.mean(k_all[n * blk:(n + 1) * blk], axis=0, keepdims=True)
        kbar = [jnp.where(tile_row == MOBA_HEAD_DIM * (1 - h) + n, mean_n, kbar[h]) for h in range(2)]
    for h in range(2):
        kaug_scr[h] = jnp.where(head_masks[h], k_all, kfeat_ref[h]).astype(BF16)
    row = lax.broadcasted_iota(jnp.int32, (blk, blk), 0)
    col = lax.broadcasted_iota(jnp.int32, (blk, blk), 1)
    causal = row >= col

    for i in range(nb):
        rows = slice(i * blk, (i + 1) * blk)
        q_blk = q_ref[0, rows, :]
        q_aug = []
        for h in range(2):
            qh = jnp.where(head_masks[h], q_blk, 0.0)
            feat = slopes[h] * (qfeat_ref[h, 0] + float(i) * qfeat_ref[h, 1])
            if i > MOBA_TOPK:
                fl = feat_lane[h]
                gate = lax.dot_general(qh, kbar[h], NT_DIMS, precision=lax.Precision.HIGHEST,
                                       preferred_element_type=F32)
                cnt = jnp.zeros((blk, LANES), F32)
                for s in range(1, i):
                    lower = pltpu.roll(gate, s, axis=1)
                    upper = pltpu.roll(gate, LANES - s, axis=1)
                    cnt = cnt + jnp.where((lower >= gate) & (fl >= s), 1.0, 0.0)
                    cnt = cnt + jnp.where((upper > gate) & (fl + s < i), 1.0, 0.0)
                dropped = (cnt >= MOBA_TOPK) & (fl >= 0) & (fl < i)
                feat = jnp.where(dropped, NEG_INF, feat)
            q_aug.append(jnp.where(head_masks[h], q_blk * (MOBA_HEAD_DIM ** -0.5), feat).astype(BF16))

        m = [jnp.full((blk, 1), NEG_INF, F32) for _ in range(2)]
        for n in range(i + 1):
            for h in range(2):
                tile = lax.dot_general(q_aug[h], kaug_scr[h, n * blk:(n + 1) * blk, :], NT_DIMS,
                                       preferred_element_type=F32)
                if n == i:
                    tile = jnp.where(causal, tile, NEG_INF)
                logit_scr[h, n] = tile
                m[h] = jnp.maximum(m[h], jnp.max(tile, axis=1, keepdims=True))
        l = [jnp.zeros((blk, 1), F32) for _ in range(2)]
        acc = jnp.zeros((2 * blk, LANES), F32)
        for n in range(i + 1):
            p = [jnp.exp(logit_scr[h, n] - m[h]) for h in range(2)]
            l = [l[h] + jnp.sum(p[h], axis=1, keepdims=True) for h in range(2)]
            acc = acc + jnp.dot(jnp.concatenate(p, axis=0).astype(BF16), vb16[n * blk:(n + 1) * blk],
                                preferred_element_type=F32)
        out = jnp.where(head_masks[0], acc[:blk] / l[0], acc[blk:] / l[1])
        o_ref[0, rows, :] = out * _silu(z_ref[0, rows, :])


def _moba(slopes, proj_a, batch, seq):
    npair = MOBA_WIDTH // LANES
    proj_a = proj_a.reshape(batch, seq, 4 * MOBA_WIDTH)
    nb = seq // MOBA_BLOCK
    assert nb <= MOBA_HEADS, "block-indicator feature lanes"
    kfeat, qfeat = _moba_features(seq)

    def col_block(base):
        return pl.BlockSpec((1, seq, LANES), lambda b, p, *_: (b, 0, base + p))

    return pl.pallas_call(
        _moba_kernel,
        grid_spec=pltpu.PrefetchScalarGridSpec(
            num_scalar_prefetch=1,
            grid=(batch, npair),
            in_specs=[col_block(0), col_block(npair), col_block(2 * npair), col_block(3 * npair),
                      pl.BlockSpec(kfeat.shape, lambda b, p, *_: (0, 0, 0)),
                      pl.BlockSpec(qfeat.shape, lambda b, p, *_: (0, 0, 0, 0))],
            out_specs=pl.BlockSpec((1, seq, LANES), lambda b, p, *_: (b, 0, p)),
            scratch_shapes=[pltpu.VMEM((2, seq, LANES), BF16),
                            pltpu.VMEM((2, nb, MOBA_BLOCK, MOBA_BLOCK), F32)],
        ),
        out_shape=jax.ShapeDtypeStruct((batch, seq, MOBA_WIDTH), F32),
        compiler_params=pltpu.CompilerParams(
            dimension_semantics=("arbitrary", "arbitrary"), vmem_limit_bytes=VMEM_LIMIT_BYTES),
        name="moba",
    )(slopes, proj_a, proj_a, proj_a, proj_a, kfeat, qfeat)


def _each(fn, *lists):
    return [fn(*args) for args in zip(*lists)]


def _unit_lower_inverse(a_list, row, col, chunk):
    def same_block(shift):
        return jnp.right_shift(row, shift) == jnp.right_shift(col, shift)

    eye = jnp.where(row == col, 1.0, 0.0)
    inner = same_block(4)
    d = _each(lambda a: jnp.where(inner, a, 0.0), a_list)
    x = _each(lambda dd: eye - dd, d)
    p = _each(_dot, d, d)
    for step in range(3):
        x = _each(lambda xx, pp: xx + _dot(xx, pp), x, p)
        if step < 2:
            p = _each(_dot, p, p)
    shift = 5
    while (1 << shift) <= chunk:
        outer = same_block(shift)
        coupling_mask = outer & jnp.logical_not(inner)
        lx = _each(lambda a, xx: _dot(jnp.where(coupling_mask, a, 0.0), xx), a_list, x)
        x = _each(lambda xx, ll: xx - _dot(xx, ll), x, lx)
        inner = outer
        shift += 1
    return x


def _gdn_kernel(alog_ref, dtb_ref, xq_ref, xk_ref, xv_ref, xz_ref, cwq_ref, cwk_ref, cwv_ref,
                ba_ref, gnw_ref, o_ref,
                pad_scr, gate_scr, q_scr, k_scr, v_scr, u_scr, lhs_scr, n_scr, egl_scr, intra_scr):
    seq = xq_ref.shape[1]
    sup = GDN_SUPER
    chunk = GDN_CHUNK
    per_sup = sup // chunk
    lhs_rows = 2 * chunk + GDN_HEAD_DIM
    head = pl.program_id(1)
    lane = lax.broadcasted_iota(jnp.int32, (1, LANES), 1)
    row = lax.broadcasted_iota(jnp.int32, (sup, sup), 0)
    col = lax.broadcasted_iota(jnp.int32, (sup, sup), 1)
    chunk_shift = chunk.bit_length() - 1
    same_chunk = jnp.right_shift(row, chunk_shift) == jnp.right_shift(col, chunk_shift)
    causal = same_chunk & (row >= col)
    strict = same_chunk & (row > col)

    @pl.when(head == 0)
    def _():
        ba = ba_ref[0]
        alog = jnp.zeros((1, LANES), F32)
        dtb = jnp.zeros((1, LANES), F32)
        for hh in range(GDN_HEADS):
            alog = jnp.where(lane == GDN_HEADS + hh, alog_ref[hh], alog)
            dtb = jnp.where(lane == GDN_HEADS + hh, dtb_ref[hh], dtb)
        beta = 1.0 / (1.0 + jnp.exp(-ba))
        pre = ba + dtb
        softplus = jnp.maximum(pre, 0.0) + jnp.log(1.0 + jnp.exp(-jnp.abs(pre)))
        g = -jnp.exp(alog) * softplus
        tri = jnp.where(causal, 1.0, 0.0)
        for s in range(seq // sup):
            rows = slice(s * sup, (s + 1) * sup)
            gc = jnp.dot(tri, g[rows], precision=lax.Precision.HIGHEST, preferred_element_type=F32)
            gate_scr[rows, :] = jnp.where(lane < GDN_HEADS, beta[rows], gc)

    def conv_silu(x_ref, cw_ref):
        pad_scr[0:8, :] = jnp.zeros((8, LANES), F32)
        pad_scr[8:, :] = x_ref[0]
        acc = pad_scr[8:8 + seq, :] * cw_ref[GDN_CONV - 1:GDN_CONV, :]
        for j in range(GDN_CONV - 1):
            shift = GDN_CONV - 1 - j
            acc = acc + pad_scr[8 - shift:8 - shift + seq, :] * cw_ref[j:j + 1, :]
        return _silu(acc)

    q = conv_silu(xq_ref, cwq_ref)
    q_scr[...] = q * lax.rsqrt(jnp.sum(q * q, axis=-1, keepdims=True) + RMS_EPS) * (GDN_HEAD_DIM ** -0.5)
    k = conv_silu(xk_ref, cwk_ref)
    k_scr[...] = k * lax.rsqrt(jnp.sum(k * k, axis=-1, keepdims=True) + RMS_EPS)
    v_scr[...] = conv_silu(xv_ref, cwv_ref)

    def prepare(s, carry):
        loaded = []
        for j in range(GDN_PREP_GROUP):
            rows = pl.ds(pl.multiple_of((s * GDN_PREP_GROUP + j) * sup, sup), sup)
            loaded.append((gate_scr[rows, :], q_scr[rows, :], k_scr[rows, :], v_scr[rows, :]))
        results = prepare_tiles(*zip(*loaded))
        for j, res in enumerate(zip(*results)):
            store_tile(s * GDN_PREP_GROUP + j, *res)
        return carry

    def decay_terms(gates):
        beta = jnp.sum(jnp.where(lane == head, gates, 0.0), axis=1, keepdims=True)
        gc = jnp.sum(jnp.where(lane == GDN_HEADS + head, gates, 0.0), axis=1, keepdims=True)
        gc_b = jnp.broadcast_to(gc, (sup, LANES))
        gc_row = jnp.concatenate([gc_b[0:LANES].T, gc_b[LANES:].T], axis=1)[0:1, :]
        decay = jnp.exp(jnp.where(causal, gc - gc_row, NEG_INF))
        gl_b = jnp.concatenate(
            [jnp.broadcast_to(gc_b[(c + 1) * chunk - 1:(c + 1) * chunk, :], (chunk, LANES))
             for c in range(per_sup)], axis=0)
        return beta, decay, jnp.exp(gc), jnp.exp(gl_b - gc_b), jnp.exp(gl_b)

    def state_terms(k_dec, w, u, q_dec, egl):
        lhs, ns, egls = [], [], []
        for c in range(per_sup):
            cr = slice(c * chunk, (c + 1) * chunk)
            pn = lax.dot_general(k_dec[cr].astype(BF16),
                                 jnp.concatenate([w[cr], u[cr]], axis=1).astype(BF16),
                                 TN_DIMS, preferred_element_type=F32)
            lhs.append(jnp.concatenate([w[cr], q_dec[cr], pn[:, :GDN_HEAD_DIM]], axis=0).astype(BF16))
            ns.append(pn[:, GDN_HEAD_DIM:])
            egls.append(egl[c * chunk:c * chunk + 8])
        return lhs, ns, egls

    def prepare_tiles(gates, qb, kb, vb):
        beta, decay, egc, e_tail, egl = zip(*_each(decay_terms, gates))
        k_beta = _each(lambda k, b: k * b, kb, beta)
        a = _each(lambda kbt, k, dc: jnp.where(strict, _dot_nt(kbt, k) * dc, 0.0), k_beta, kb, decay)
        t_inv = _unit_lower_inverse(a, row, col, chunk)
        u = _each(lambda t, v, b: _dot(t, v * b), t_inv, vb, beta)
        w = _each(lambda t, kbt, e: _dot(t, kbt * e), t_inv, k_beta, egc)
        intra = _each(lambda q, k, dc: jnp.where(causal, _dot_nt(q, k) * dc, 0.0).astype(BF16),
                      qb, kb, decay)
        q_dec = _each(lambda q, e: q * e, qb, egc)
        k_dec = _each(lambda k, e: k * e, kb, e_tail)
        lhs, ns, egls = zip(*_each(state_terms, k_dec, w, u, q_dec, egl))
        return u, intra, lhs, ns, egls

    def store_tile(s, u, intra, lhs, ns, egls):
        rows = pl.ds(pl.multiple_of(s * sup, sup), sup)
        u_scr[rows, :] = u
        intra_scr[rows, :] = intra
        for c in range(per_sup):
            idx = s * per_sup + c
            lhs_scr[pl.ds(pl.multiple_of(idx * lhs_rows, LANES), lhs_rows), :] = lhs[c]
            n_scr[pl.ds(pl.multiple_of(idx * GDN_HEAD_DIM, GDN_HEAD_DIM), GDN_HEAD_DIM), :] = ns[c]
            egl_scr[pl.ds(pl.multiple_of(idx * 8, 8), 8), :] = egls[c]

    lax.fori_loop(0, seq // (sup * GDN_PREP_GROUP), prepare, 0)

    gnw = gnw_ref[...]

    def scan(s, state):
        r0 = pl.multiple_of(s * sup, sup)
        v_parts = []
        for c in range(per_sup):
            rows = pl.ds(r0 + c * chunk, chunk)
            idx = s * per_sup + c
            lhs = lhs_scr[pl.ds(pl.multiple_of(idx * lhs_rows, LANES), lhs_rows), :]
            r = jnp.dot(lhs, state.astype(BF16), preferred_element_type=F32)
            v_new = u_scr[rows, :] - r[:chunk]
            v_parts.append(v_new)
            v_sup = jnp.concatenate(
                v_parts + [jnp.zeros(((per_sup - 1 - c) * chunk, LANES), F32)] * (c < per_sup - 1), axis=0)
            o = r[chunk:2 * chunk] + jnp.dot(intra_scr[rows, :], v_sup.astype(BF16),
                                             preferred_element_type=F32)
            egl = jnp.broadcast_to(egl_scr[pl.ds(pl.multiple_of(idx * 8, 8), 8), :][0:1, :],
                                   (GDN_HEAD_DIM, LANES))
            state = (state * egl - r[2 * chunk:]
                     + n_scr[pl.ds(pl.multiple_of(idx * GDN_HEAD_DIM, GDN_HEAD_DIM), GDN_HEAD_DIM), :])
            y = o * lax.rsqrt(jnp.mean(o * o, axis=-1, keepdims=True) + RMS_EPS) * gnw
            o_ref[0, rows, :] = y * _silu(xz_ref[0, rows, :])
        return state

    lax.fori_loop(0, seq // sup, scan, jnp.zeros((GDN_HEAD_DIM, GDN_HEAD_DIM), F32))


def _gdn(a_log, dt_bias, proj_d, gates, conv_w, gdn_norm_w, batch, seq):
    nh = GDN_HEADS
    proj_d = proj_d.reshape(batch, seq, 4 * GDN_WIDTH)
    gates = gates.reshape(batch, seq, LANES)

    def col_block(base):
        return pl.BlockSpec((1, seq, LANES), lambda b, h, *_: (b, 0, base + h))

    def conv_block(base):
        return pl.BlockSpec((GDN_CONV, LANES), lambda b, h, *_: (0, base + h))

    tile = pltpu.VMEM((seq, LANES), F32)
    nchunk = seq // GDN_CHUNK
    scratch = [
        pltpu.VMEM((seq + 8, LANES), F32),
        tile, tile, tile, tile, tile,
        pltpu.VMEM((nchunk * (2 * GDN_CHUNK + GDN_HEAD_DIM), LANES), BF16),
        pltpu.VMEM((nchunk * GDN_HEAD_DIM, LANES), F32),
        pltpu.VMEM((nchunk * 8, LANES), F32),
        pltpu.VMEM((seq, GDN_SUPER), BF16),
    ]
    return pl.pallas_call(
        _gdn_kernel,
        grid_spec=pltpu.PrefetchScalarGridSpec(
            num_scalar_prefetch=2,
            grid=(batch, nh),
            in_specs=[col_block(0), col_block(nh), col_block(2 * nh), col_block(3 * nh),
                      conv_block(0), conv_block(nh), conv_block(2 * nh),
                      pl.BlockSpec((1, seq, LANES), lambda b, h, *_: (b, 0, 0)),
                      pl.BlockSpec((1, LANES), lambda b, h, *_: (0, 0))],
            out_specs=pl.BlockSpec((1, seq, LANES), lambda b, h, *_: (b, 0, h)),
            scratch_shapes=scratch,
        ),
        out_shape=jax.ShapeDtypeStruct((batch, seq, GDN_WIDTH), F32),
        compiler_params=pltpu.CompilerParams(
            dimension_semantics=("arbitrary", "arbitrary"), vmem_limit_bytes=VMEM_LIMIT_BYTES),
        name="gdn",
    )(a_log, dt_bias, proj_d, proj_d, proj_d, proj_d, conv_w, conv_w, conv_w, gates, gdn_norm_w)


def _outproj_kernel(x_ref, ma_ref, md_ref, wa_ref, wd_ref, nw_ref, o_ref):
    y = (jnp.dot(ma_ref[...].astype(BF16), wa_ref[...], preferred_element_type=F32)
         + jnp.dot(md_ref[...].astype(BF16), wd_ref[...], preferred_element_type=F32))
    y = y * lax.rsqrt(jnp.mean(y * y, axis=-1, keepdims=True) + RMS_EPS) * nw_ref[...]
    o_ref[...] = x_ref[...] + y


def _outproj(x2, mix_a, mix_d, w_a, w_d, norm_w, tm):
    n = x2.shape[0]
    return pl.pallas_call(
        _outproj_kernel,
        grid=(n // tm,),
        in_specs=[
            pl.BlockSpec((tm, D_MODEL), lambda i: (i, 0)),
            pl.BlockSpec((tm, MOBA_WIDTH), lambda i: (i, 0)),
            pl.BlockSpec((tm, GDN_WIDTH), lambda i: (i, 0)),
            pl.BlockSpec(w_a.shape, lambda i: (0, 0)),
            pl.BlockSpec(w_d.shape, lambda i: (0, 0)),
            pl.BlockSpec((1, D_MODEL), lambda i: (0, 0)),
        ],
        out_specs=pl.BlockSpec((tm, D_MODEL), lambda i: (i, 0)),
        out_shape=jax.ShapeDtypeStruct((n, D_MODEL), F32),
        compiler_params=pltpu.CompilerParams(
            dimension_semantics=("arbitrary",), vmem_limit_bytes=VMEM_LIMIT_BYTES),
        name="outproj",
    )(x2, mix_a, mix_d, w_a, w_d, norm_w)


def _alibi_slopes():
    start = 2.0 ** (-8.0 / MOBA_HEADS)
    slopes = np.asarray(start ** np.arange(1, MOBA_HEADS + 1), dtype=np.float32)
    return jnp.asarray(slopes)


def _layer(x, norm_pre_w, w_in, conv_w, a_log, dt_bias, gdn_norm_w, w_out, norm_post_w):
    batch, seq, _ = x.shape
    assert seq % MOBA_BLOCK == 0 and seq % GDN_SUPER == 0
    n = batch * seq
    tm = 512
    assert n % tm == 0
    x2 = x.reshape(n, D_MODEL)
    n_main = 4 * MOBA_WIDTH + 4 * GDN_WIDTH
    w_main = w_in[:, :n_main].astype(BF16)
    w_tail = jnp.pad(w_in[:, n_main:], ((0, 0), (0, LANES - 2 * GDN_HEADS))).astype(BF16)
    proj_a, proj_d, gates = _inproj(x2, norm_pre_w.reshape(1, D_MODEL), w_main, w_tail, tm)
    mix_a = _moba(_alibi_slopes(), proj_a, batch, seq)
    mix_d = _gdn(a_log.astype(F32), dt_bias.astype(F32), proj_d, gates, conv_w,
                 gdn_norm_w.reshape(1, GDN_HEAD_DIM), batch, seq)
    w_o = w_out.astype(BF16)
    out = _outproj(x2, mix_a.reshape(n, MOBA_WIDTH), mix_d.reshape(n, GDN_WIDTH),
                   w_o[:MOBA_WIDTH], w_o[MOBA_WIDTH:], norm_post_w.reshape(1, D_MODEL), tm)
    return out.reshape(batch, seq, D_MODEL)


def kernel(x, norm_pre_w, w_in, conv_w, a_log, dt_bias, gdn_norm_w, w_out, norm_post_w):
    for layer in range(norm_pre_w.shape[0]):
        x = _layer(x, norm_pre_w[layer], w_in[layer], conv_w[layer], a_log[layer], dt_bias[layer],
                   gdn_norm_w[layer], w_out[layer], norm_post_w[layer])
    return x
```

```python
import functools

import numpy as np
import jax
import jax.numpy as jnp
from jax import lax
from jax.experimental import pallas as pl
from jax.experimental.pallas import tpu as pltpu

F32 = jnp.float32
BF16 = jnp.bfloat16

D_MODEL = 1024
MOBA_WIDTH = 512
MOBA_HEAD_DIM = 64
MOBA_HEADS = MOBA_WIDTH // MOBA_HEAD_DIM
MOBA_BLOCK = 256
MOBA_TOPK = 3
GDN_WIDTH = 512
GDN_HEAD_DIM = 128
GDN_HEADS = GDN_WIDTH // GDN_HEAD_DIM
GDN_CONV = 4
GDN_CHUNK = 128
GDN_SUPER = 256
GDN_HEADS_PER_STEP = 2
GDN_PREP_GROUP = 4
RMS_EPS = 1e-6
NEG_INF = -1e30

LANES = 128
VMEM_LIMIT_BYTES = 48 * 1024 * 1024

NT_DIMS = (((1,), (1,)), ((), ()))
TN_DIMS = (((0,), (0,)), ((), ()))


def _dot(a, b):
    return jnp.dot(a.astype(BF16), b.astype(BF16), preferred_element_type=F32)


def _dot_nt(a, b):
    return lax.dot_general(a.astype(BF16), b.astype(BF16), NT_DIMS, preferred_element_type=F32)


def _silu(x):
    return x / (1.0 + jnp.exp(-x))


def _inproj_kernel(x_ref, nw_ref, wm_ref, wt_ref, moba_ref, gdn_ref, ba_ref):
    x = x_ref[...]
    h = x * lax.rsqrt(jnp.mean(x * x, axis=-1, keepdims=True) + RMS_EPS) * nw_ref[...]
    hb = h.astype(BF16)
    half = moba_ref.shape[-1]
    moba_ref[...] = jnp.dot(hb, wm_ref[:, :half], preferred_element_type=F32)
    gdn_ref[...] = jnp.dot(hb, wm_ref[:, half:], preferred_element_type=F32)
    ba_ref[...] = jnp.dot(hb, wt_ref[...], preferred_element_type=F32)


def _inproj(x2, norm_w, w_main, w_tail, tm):
    n = x2.shape[0]
    half = w_main.shape[1] // 2
    return pl.pallas_call(
        _inproj_kernel,
        grid=(n // tm,),
        in_specs=[
            pl.BlockSpec((tm, D_MODEL), lambda i: (i, 0)),
            pl.BlockSpec((1, D_MODEL), lambda i: (0, 0)),
            pl.BlockSpec(w_main.shape, lambda i: (0, 0)),
            pl.BlockSpec(w_tail.shape, lambda i: (0, 0)),
        ],
        out_specs=[
            pl.BlockSpec((tm, half), lambda i: (i, 0)),
            pl.BlockSpec((tm, half), lambda i: (i, 0)),
            pl.BlockSpec((tm, LANES), lambda i: (i, 0)),
        ],
        out_shape=[
            jax.ShapeDtypeStruct((n, half), F32),
            jax.ShapeDtypeStruct((n, half), F32),
            jax.ShapeDtypeStruct((n, LANES), F32),
        ],
        compiler_params=pltpu.CompilerParams(
            dimension_semantics=("arbitrary",), vmem_limit_bytes=VMEM_LIMIT_BYTES),
        name="inproj",
    )(x2, norm_w, w_main, w_tail)


FEAT_POS_HI = MOBA_HEADS
FEAT_POS_LO = MOBA_HEADS + 1
FEAT_ONE_HI = MOBA_HEADS + 2
FEAT_ONE_LO = MOBA_HEADS + 3


def _moba_features(seq):
    blk = MOBA_BLOCK
    pos = np.arange(seq)
    kfeat = np.zeros((2, seq, LANES), np.float32)
    qfeat = np.zeros((2, 2, blk, LANES), np.float32)
    for h in range(2):
        base = MOBA_HEAD_DIM * (1 - h)
        kfeat[h, pos, base + pos // blk] = 1.0
        kfeat[h, :, base + FEAT_POS_HI] = blk * (pos // blk)
        kfeat[h, :, base + FEAT_POS_LO] = pos % blk
        kfeat[h, :, base + FEAT_ONE_HI] = 1.0
        kfeat[h, :, base + FEAT_ONE_LO] = 1.0
        qfeat[h, 0, :, base + FEAT_POS_HI] = 1.0
        qfeat[h, 0, :, base + FEAT_POS_LO] = 1.0
        qfeat[h, 0, :, base + FEAT_ONE_LO] = -np.arange(blk)
        qfeat[h, 1, :, base + FEAT_ONE_HI] = -blk
    return jnp.asarray(kfeat), jnp.asarray(qfeat)


def _moba_kernel(slopes_ref, q_ref, k_ref, v_ref, z_ref, kfeat_ref, qfeat_ref, o_ref,
                 kaug_scr, logit_scr):
    seq = q_ref.shape[1]
    blk = MOBA_BLOCK
    nb = seq // blk
    pair = pl.program_id(1)
    lane = lax.broadcasted_iota(jnp.int32, (1, LANES), 1)
    head_masks = [(lane >= MOBA_HEAD_DIM * h) & (lane < MOBA_HEAD_DIM * (h + 1)) for h in range(2)]
    feat_lane = [lane - MOBA_HEAD_DIM * (1 - h) for h in range(2)]
    slopes = [slopes_ref[pair * 2 + h] for h in range(2)]

    k_all = k_ref[0]
    vb16 = v_ref[0].astype(BF16)
    tile_row = lax.broadcasted_iota(jnp.int32, (LANES, LANES), 0)
    kbar = [jnp.zeros((LANES, LANES), F32) for _ in range(2)]
    for n in range(nb):
        mean_n = jnp.mean(k_all[n * blk:(n + 1) * blk], axis=0, keepdims=True)
        kbar = [jnp.where(tile_row == MOBA_HEAD_DIM * (1 - h) + n, mean_n, kbar[h]) for h in range(2)]
    for h in range(2):
        kaug_scr[h] = jnp.where(head_masks[h], k_all, kfeat_ref[h]).astype(BF16)
    row = lax.broadcasted_iota(jnp.int32, (blk, blk), 0)
    col = lax.broadcasted_iota(jnp.int32, (blk, blk), 1)
    causal = row >= col

    for i in range(nb):
        rows = slice(i * blk, (i + 1) * blk)
        q_blk = q_ref[0, rows, :]
        q_aug = []
        for h in range(2):
            qh = jnp.where(head_masks[h], q_blk, 0.0)
            feat = slopes[h] * (qfeat_ref[h, 0] + float(i) * qfeat_ref[h, 1])
            if i > MOBA_TOPK:
                fl = feat_lane[h]
                gate = lax.dot_general(qh, kbar[h], NT_DIMS, precision=lax.Precision.HIGHEST,
                                       preferred_element_type=F32)
                cnt = jnp.zeros((blk, LANES), F32)
                for s in range(1, i):
                    lower = pltpu.roll(gate, s, axis=1)
                    upper = pltpu.roll(gate, LANES - s, axis=1)
                    cnt = cnt + jnp.where((lower >= gate) & (fl >= s), 1.0, 0.0)
                    cnt = cnt + jnp.where((upper > gate) & (fl + s < i), 1.0, 0.0)
                dropped = (cnt >= MOBA_TOPK) & (fl >= 0) & (fl < i)
                feat = jnp.where(dropped, NEG_INF, feat)
            q_aug.append(jnp.where(head_masks[h], q_blk * (MOBA_HEAD_DIM ** -0.5), feat).astype(BF16))

        m = [jnp.full((blk, 1), NEG_INF, F32) for _ in range(2)]
        for n in range(i + 1):
            for h in range(2):
                tile = lax.dot_general(q_aug[h], kaug_scr[h, n * blk:(n + 1) * blk, :], NT_DIMS,
                                       preferred_element_type=F32)
                if n == i:
                    tile = jnp.where(causal, tile, NEG_INF)
                logit_scr[h, n] = tile
                m[h] = jnp.maximum(m[h], jnp.max(tile, axis=1, keepdims=True))
        l = [jnp.zeros((blk, 1), F32) for _ in range(2)]
        acc = jnp.zeros((2 * blk, LANES), F32)
        for n in range(i + 1):
            p = [jnp.exp(logit_scr[h, n] - m[h]) for h in range(2)]
            l = [l[h] + jnp.sum(p[h], axis=1, keepdims=True) for h in range(2)]
            acc = acc + jnp.dot(jnp.concatenate(p, axis=0).astype(BF16), vb16[n * blk:(n + 1) * blk],
                                preferred_element_type=F32)
        out = jnp.where(head_masks[0], acc[:blk] / l[0], acc[blk:] / l[1])
        o_ref[0, rows, :] = out * _silu(z_ref[0, rows, :])


def _moba(slopes, proj_a, batch, seq):
    npair = MOBA_WIDTH // LANES
    proj_a = proj_a.reshape(batch, seq, 4 * MOBA_WIDTH)
    nb = seq // MOBA_BLOCK
    assert nb <= MOBA_HEADS, "block-indicator feature lanes"
    kfeat, qfeat = _moba_features(seq)

    def col_block(base):
        return pl.BlockSpec((1, seq, LANES), lambda b, p, *_: (b, 0, base + p))

    return pl.pallas_call(
        _moba_kernel,
        grid_spec=pltpu.PrefetchScalarGridSpec(
            num_scalar_prefetch=1,
            grid=(batch, npair),
            in_specs=[col_block(0), col_block(npair), col_block(2 * npair), col_block(3 * npair),
                      pl.BlockSpec(kfeat.shape, lambda b, p, *_: (0, 0, 0)),
                      pl.BlockSpec(qfeat.shape, lambda b, p, *_: (0, 0, 0, 0))],
            out_specs=pl.BlockSpec((1, seq, LANES), lambda b, p, *_: (b, 0, p)),
            scratch_shapes=[pltpu.VMEM((2, seq, LANES), BF16),
                            pltpu.VMEM((2, nb, MOBA_BLOCK, MOBA_BLOCK), F32)],
        ),
        out_shape=jax.ShapeDtypeStruct((batch, seq, MOBA_WIDTH), F32),
        compiler_params=pltpu.CompilerParams(
            dimension_semantics=("arbitrary", "arbitrary"), vmem_limit_bytes=VMEM_LIMIT_BYTES),
        name="moba",
    )(slopes, proj_a, proj_a, proj_a, proj_a, kfeat, qfeat)


def _each(fn, *lists):
    return [fn(*args) for args in zip(*lists)]


def _unit_lower_inverse(a_list, row, col, chunk):
    def same_block(shift):
        return jnp.right_shift(row, shift) == jnp.right_shift(col, shift)

    eye = jnp.where(row == col, 1.0, 0.0)
    inner = same_block(4)
    d = _each(lambda a: jnp.where(inner, a, 0.0), a_list)
    x = _each(lambda dd: eye - dd, d)
    p = _each(_dot, d, d)
    for step in range(3):
        x = _each(lambda xx, pp: xx + _dot(xx, pp), x, p)
        if step < 2:
            p = _each(_dot, p, p)
    shift = 5
    while (1 << shift) <= chunk:
        outer = same_block(shift)
        coupling_mask = outer & jnp.logical_not(inner)
        lx = _each(lambda a, xx: _dot(jnp.where(coupling_mask, a, 0.0), xx), a_list, x)
        x = _each(lambda xx, ll: xx - _dot(xx, ll), x, lx)
        inner = outer
        shift += 1
    return x


def _gdn_kernel(alog_ref, dtb_ref, xq_ref, xk_ref, xv_ref, xz_ref, cwq_ref, cwk_ref, cwv_ref,
                ba_ref, gnw_ref, o_ref,
                pad_scr, gate_scr, q_scr, k_scr, v_scr, u_scr, lhs_scr, n_scr, egl_scr, intra_scr):
    seq = xq_ref.shape[1]
    sup = GDN_SUPER
    chunk = GDN_CHUNK
    per_sup = sup // chunk
    nsup = seq // sup
    hps = GDN_HEADS_PER_STEP
    lhs_rows = 2 * chunk + GDN_HEAD_DIM
    group = pl.program_id(1)
    lane = lax.broadcasted_iota(jnp.int32, (1, LANES), 1)
    row = lax.broadcasted_iota(jnp.int32, (sup, sup), 0)
    col = lax.broadcasted_iota(jnp.int32, (sup, sup), 1)
    chunk_shift = chunk.bit_length() - 1
    same_chunk = jnp.right_shift(row, chunk_shift) == jnp.right_shift(col, chunk_shift)
    causal = same_chunk & (row >= col)
    strict = same_chunk & (row > col)

    @pl.when(group == 0)
    def _():
        ba = ba_ref[0]
        alog = jnp.zeros((1, LANES), F32)
        dtb = jnp.zeros((1, LANES), F32)
        for hh in range(GDN_HEADS):
            alog = jnp.where(lane == GDN_HEADS + hh, alog_ref[hh], alog)
            dtb = jnp.where(lane == GDN_HEADS + hh, dtb_ref[hh], dtb)
        beta = 1.0 / (1.0 + jnp.exp(-ba))
        pre = ba + dtb
        softplus = jnp.maximum(pre, 0.0) + jnp.log(1.0 + jnp.exp(-jnp.abs(pre)))
        g = -jnp.exp(alog) * softplus
        tri = jnp.where(causal, 1.0, 0.0)
        for s in range(nsup):
            rows = slice(s * sup, (s + 1) * sup)
            gc = jnp.dot(tri, g[rows], precision=lax.Precision.HIGHEST, preferred_element_type=F32)
            gate_scr[rows, :] = jnp.where(lane < GDN_HEADS, beta[rows], gc)

    def conv_silu(x_ref, cw_ref):
        pad_scr[0:8, :] = jnp.zeros((8, x_ref.shape[2]), F32)
        pad_scr[8:, :] = x_ref[0]
        acc = pad_scr[8:8 + seq, :] * cw_ref[GDN_CONV - 1:GDN_CONV, :]
        for j in range(GDN_CONV - 1):
            shift = GDN_CONV - 1 - j
            acc = acc + pad_scr[8 - shift:8 - shift + seq, :] * cw_ref[j:j + 1, :]
        return _silu(acc)

    def l2_normalize(x):
        return x * lax.rsqrt(jnp.sum(x * x, axis=-1, keepdims=True) + RMS_EPS)

    q = conv_silu(xq_ref, cwq_ref)
    for j in range(hps):
        q_scr[j] = l2_normalize(q[:, j * LANES:(j + 1) * LANES]) * (GDN_HEAD_DIM ** -0.5)
    k = conv_silu(xk_ref, cwk_ref)
    for j in range(hps):
        k_scr[j] = l2_normalize(k[:, j * LANES:(j + 1) * LANES])
    v = conv_silu(xv_ref, cwv_ref)
    for j in range(hps):
        v_scr[j] = v[:, j * LANES:(j + 1) * LANES]

    tiles_per_step = GDN_PREP_GROUP // hps

    def prepare(step, carry):
        loaded = []
        for t in range(tiles_per_step):
            rows = pl.ds(pl.multiple_of((step * tiles_per_step + t) * sup, sup), sup)
            for j in range(hps):
                loaded.append((group * hps + j, gate_scr[rows, :], q_scr[j, rows, :], k_scr[j, rows, :],
                               v_scr[j, rows, :]))
        results = prepare_tiles(*zip(*loaded))
        for t in range(tiles_per_step):
            for j in range(hps):
                store_tile(j, step * tiles_per_step + t, *[r[t * hps + j] for r in results])
        return carry

    def decay_terms(head, gates):
        beta = jnp.sum(jnp.where(lane == head, gates, 0.0), axis=1, keepdims=True)
        gc = jnp.sum(jnp.where(lane == GDN_HEADS + head, gates, 0.0), axis=1, keepdims=True)
        gc_b = jnp.broadcast_to(gc, (sup, LANES))
        gc_row = jnp.concatenate([gc_b[0:LANES].T, gc_b[LANES:].T], axis=1)[0:1, :]
        decay = jnp.exp(jnp.where(causal, gc - gc_row, NEG_INF))
        gl_b = jnp.concatenate(
            [jnp.broadcast_to(gc_b[(c + 1) * chunk - 1:(c + 1) * chunk, :], (chunk, LANES))
             for c in range(per_sup)], axis=0)
        return beta, decay, jnp.exp(gc), jnp.exp(gl_b - gc_b), jnp.exp(gl_b)

    def state_terms(k_dec, w, u, q_dec, egl):
        lhs, ns, egls = [], [], []
        for c in range(per_sup):
            cr = slice(c * chunk, (c + 1) * chunk)
            pn = lax.dot_general(k_dec[cr].astype(BF16),
                                 jnp.concatenate([w[cr], u[cr]], axis=1).astype(BF16),
                                 TN_DIMS, preferred_element_type=F32)
            lhs.append(jnp.concatenate([w[cr], q_dec[cr], pn[:, :GDN_HEAD_DIM]], axis=0).astype(BF16))
            ns.append(pn[:, GDN_HEAD_DIM:])
            egls.append(egl[c * chunk:c * chunk + 8])
        return lhs, ns, egls

    def prepare_tiles(heads, gates, qb, kb, vb):
        beta, decay, egc, e_tail, egl = zip(*_each(decay_terms, heads, gates))
        k_beta = _each(lambda k, b: k * b, kb, beta)
        a = _each(lambda kbt, k, dc: jnp.where(strict, _dot_nt(kbt, k) * dc, 0.0), k_beta, kb, decay)
        t_inv = _unit_lower_inverse(a, row, col, chunk)
        u = _each(lambda t, v, b: _dot(t, v * b), t_inv, vb, beta)
        w = _each(lambda t, kbt, e: _dot(t, kbt * e), t_inv, k_beta, egc)
        intra = _each(lambda q, k, dc: jnp.where(causal, _dot_nt(q, k) * dc, 0.0).astype(BF16),
                      qb, kb, decay)
        q_dec = _each(lambda q, e: q * e, qb, egc)
        k_dec = _each(lambda k, e: k * e, kb, e_tail)
        lhs, ns, egls = zip(*_each(state_terms, k_dec, w, u, q_dec, egl))
        return u, intra, lhs, ns, egls

    def chunk_slot(s, c, size):
        return pl.ds(pl.multiple_of((s * per_sup + c) * size, size), size)

    def store_tile(j, s, u, intra, lhs, ns, egls):
        rows = pl.ds(pl.multiple_of(s * sup, sup), sup)
        u_scr[j, rows, :] = u
        intra_scr[j, rows, :] = intra
        for c in range(per_sup):
            lhs_scr[j, chunk_slot(s, c, lhs_rows), :] = lhs[c]
            n_scr[j, chunk_slot(s, c, GDN_HEAD_DIM), :] = ns[c]
            egl_scr[j, chunk_slot(s, c, 8), :] = egls[c]

    lax.fori_loop(0, nsup // tiles_per_step, prepare, 0)

    gnw = gnw_ref[...]

    def scan(s, states):
        r0 = pl.multiple_of(s * sup, sup)
        states = list(states)
        v_parts = [[] for _ in range(hps)]
        for c in range(per_sup):
            rows = pl.ds(r0 + c * chunk, chunk)
            r = [jnp.dot(lhs_scr[j, chunk_slot(s, c, lhs_rows), :], states[j].astype(BF16),
                         preferred_element_type=F32) for j in range(hps)]
            outs = []
            for j in range(hps):
                v_parts[j].append(u_scr[j, rows, :] - r[j][:chunk])
                v_sup = jnp.concatenate(
                    v_parts[j] + [jnp.zeros(((per_sup - 1 - c) * chunk, LANES), F32)] * (c < per_sup - 1),
                    axis=0)
                o = r[j][chunk:2 * chunk] + jnp.dot(intra_scr[j, rows, :], v_sup.astype(BF16),
                                                    preferred_element_type=F32)
                egl = jnp.broadcast_to(egl_scr[j, chunk_slot(s, c, 8), :][0:1, :], (GDN_HEAD_DIM, LANES))
                states[j] = states[j] * egl - r[j][2 * chunk:] + n_scr[j, chunk_slot(s, c, GDN_HEAD_DIM), :]
                outs.append(o * lax.rsqrt(jnp.mean(o * o, axis=-1, keepdims=True) + RMS_EPS) * gnw)
            o_ref[0, rows, :] = jnp.concatenate(outs, axis=1) * _silu(xz_ref[0, rows, :])
        return tuple(states)

    lax.fori_loop(0, nsup, scan, tuple(jnp.zeros((GDN_HEAD_DIM, GDN_HEAD_DIM), F32) for _ in range(hps)))


def _gdn(a_log, dt_bias, proj_d, gates, conv_w, gdn_norm_w, batch, seq):
    hps = GDN_HEADS_PER_STEP
    ngroup = GDN_HEADS // hps
    width = hps * GDN_HEAD_DIM
    assert GDN_PREP_GROUP % hps == 0 and (seq // GDN_SUPER) % (GDN_PREP_GROUP // hps) == 0
    proj_d = proj_d.reshape(batch, seq, 4 * GDN_WIDTH)
    gates = gates.reshape(batch, seq, LANES)

    def col_block(base):
        return pl.BlockSpec((1, seq, width), lambda b, g, *_: (b, 0, base + g))

    def conv_block(base):
        return pl.BlockSpec((GDN_CONV, width), lambda b, g, *_: (0, base + g))

    def per_head(rows, lanes, dtype):
        return pltpu.VMEM((hps, rows, lanes), dtype)

    nchunk = seq // GDN_CHUNK
    scratch = [
        pltpu.VMEM((seq + 8, width), F32),
        pltpu.VMEM((seq, LANES), F32),
        per_head(seq, LANES, F32), per_head(seq, LANES, F32), per_head(seq, LANES, F32),
        per_head(seq, LANES, F32),
        per_head(nchunk * (2 * GDN_CHUNK + GDN_HEAD_DIM), LANES, BF16),
        per_head(nchunk * GDN_HEAD_DIM, LANES, F32),
        per_head(nchunk * 8, LANES, F32),
        per_head(seq, GDN_SUPER, BF16),
    ]
    return pl.pallas_call(
        _gdn_kernel,
        grid_spec=pltpu.PrefetchScalarGridSpec(
            num_scalar_prefetch=2,
            grid=(batch, ngroup),
            in_specs=[col_block(0), col_block(ngroup), col_block(2 * ngroup), col_block(3 * ngroup),
                      conv_block(0), conv_block(ngroup), conv_block(2 * ngroup),
                      pl.BlockSpec((1, seq, LANES), lambda b, g, *_: (b, 0, 0)),
                      pl.BlockSpec((1, LANES), lambda b, g, *_: (0, 0))],
            out_specs=pl.BlockSpec((1, seq, width), lambda b, g, *_: (b, 0, g)),
            scratch_shapes=scratch,
        ),
        out_shape=jax.ShapeDtypeStruct((batch, seq, GDN_WIDTH), F32),
        compiler_params=pltpu.CompilerParams(
            dimension_semantics=("arbitrary", "arbitrary"), vmem_limit_bytes=VMEM_LIMIT_BYTES),
        name="gdn",
    )(a_log, dt_bias, proj_d, proj_d, proj_d, proj_d, conv_w, conv_w, conv_w, gates, gdn_norm_w)


def _outproj_kernel(x_ref, ma_ref, md_ref, wa_ref, wd_ref, nw_ref, o_ref):
    y = (jnp.dot(ma_ref[...].astype(BF16), wa_ref[...], preferred_element_type=F32)
         + jnp.dot(md_ref[...].astype(BF16), wd_ref[...], preferred_element_type=F32))
    y = y * lax.rsqrt(jnp.mean(y * y, axis=-1, keepdims=True) + RMS_EPS) * nw_ref[...]
    o_ref[...] = x_ref[...] + y


def _outproj(x2, mix_a, mix_d, w_a, w_d, norm_w, tm):
    n = x2.shape[0]
    return pl.pallas_call(
        _outproj_kernel,
        grid=(n // tm,),
        in_specs=[
            pl.BlockSpec((tm, D_MODEL), lambda i: (i, 0)),
            pl.BlockSpec((tm, MOBA_WIDTH), lambda i: (i, 0)),
            pl.BlockSpec((tm, GDN_WIDTH), lambda i: (i, 0)),
            pl.BlockSpec(w_a.shape, lambda i: (0, 0)),
            pl.BlockSpec(w_d.shape, lambda i: (0, 0)),
            pl.BlockSpec((1, D_MODEL), lambda i: (0, 0)),
        ],
        out_specs=pl.BlockSpec((tm, D_MODEL), lambda i: (i, 0)),
        out_shape=jax.ShapeDtypeStruct((n, D_MODEL), F32),
        compiler_params=pltpu.CompilerParams(
            dimension_semantics=("arbitrary",), vmem_limit_bytes=VMEM_LIMIT_BYTES),
        name="outproj",
    )(x2, mix_a, mix_d, w_a, w_d, norm_w)


def _alibi_slopes():
    start = 2.0 ** (-8.0 / MOBA_HEADS)
    slopes = np.asarray(start ** np.arange(1, MOBA_HEADS + 1), dtype=np.float32)
    return jnp.asarray(slopes)


def _layer(x, norm_pre_w, w_in, conv_w, a_log, dt_bias, gdn_norm_w, w_out, norm_post_w):
    batch, seq, _ = x.shape
    assert seq % MOBA_BLOCK == 0 and seq % GDN_SUPER == 0
    n = batch * seq
    tm = 512
    assert n % tm == 0
    x2 = x.reshape(n, D_MODEL)
    n_main = 4 * MOBA_WIDTH + 4 * GDN_WIDTH
    w_main = w_in[:, :n_main].astype(BF16)
    w_tail = jnp.pad(w_in[:, n_main:], ((0, 0), (0, LANES - 2 * GDN_HEADS))).astype(BF16)
    proj_a, proj_d, gates = _inproj(x2, norm_pre_w.reshape(1, D_MODEL), w_main, w_tail, tm)
    mix_a = _moba(_alibi_slopes(), proj_a, batch, seq)
    mix_d = _gdn(a_log.astype(F32), dt_bias.astype(F32), proj_d, gates, conv_w,
                 gdn_norm_w.reshape(1, GDN_HEAD_DIM), batch, seq)
    w_o = w_out.astype(BF16)
    out = _outproj(x2, mix_a.reshape(n, MOBA_WIDTH), mix_d.reshape(n, GDN_WIDTH),
                   w_o[:MOBA_WIDTH], w_o[MOBA_WIDTH:], norm_post_w.reshape(1, D_MODEL), tm)
    return out.reshape(batch, seq, D_MODEL)


def kernel(x, norm_pre_w, w_in, conv_w, a_log, dt_bias, gdn_norm_w, w_out, norm_post_w):
    for layer in range(norm_pre_w.shape[0]):
        x = _layer(x, norm_pre_w[layer], w_in[layer], conv_w[layer], a_log[layer], dt_bias[layer],
                   gdn_norm_w[layer], w_out[layer], norm_post_w[layer])
    return x
```

```python
import functools

import numpy as np
import jax
import jax.numpy as jnp
from jax import lax
from jax.experimental import pallas as pl
from jax.experimental.pallas import tpu as pltpu

F32 = jnp.float32
BF16 = jnp.bfloat16

D_MODEL = 1024
MOBA_WIDTH = 512
MOBA_HEAD_DIM = 64
MOBA_HEADS = MOBA_WIDTH // MOBA_HEAD_DIM
MOBA_BLOCK = 256
MOBA_TOPK = 3
GDN_WIDTH = 512
GDN_HEAD_DIM = 128
GDN_HEADS = GDN_WIDTH // GDN_HEAD_DIM
GDN_CONV = 4
GDN_CHUNK = 128
GDN_SUPER = 256
GDN_HEADS_PER_STEP = 2
GDN_PREP_GROUP = 4
RMS_EPS = 1e-6
NEG_INF = -1e30

LANES = 128
VMEM_LIMIT_BYTES = 48 * 1024 * 1024

NT_DIMS = (((1,), (1,)), ((), ()))
TN_DIMS = (((0,), (0,)), ((), ()))


def _dot(a, b):
    return jnp.dot(a.astype(BF16), b.astype(BF16), preferred_element_type=F32)


def _dot_nt(a, b):
    return lax.dot_general(a.astype(BF16), b.astype(BF16), NT_DIMS, preferred_element_type=F32)


def _silu(x):
    return x / (1.0 + jnp.exp(-x))


def _l2_normalize(x):
    return x * lax.rsqrt(jnp.sum(x * x, axis=-1, keepdims=True) + RMS_EPS)


def _inproj_kernel(tiles_per_seq, x_ref, nw_ref, wm_ref, wt_ref, cw_ref, alog_ref, dtb_ref,
                   moba_ref, gdn_ref, gate_ref, pad_scr):
    tm = x_ref.shape[0]
    x = x_ref[...]
    h = x * lax.rsqrt(jnp.mean(x * x, axis=-1, keepdims=True) + RMS_EPS) * nw_ref[...]
    hb = h.astype(BF16)
    half = moba_ref.shape[-1]

    conv_width = pad_scr.shape[1]
    first = (pl.program_id(0) % tiles_per_seq) == 0

    @pl.when(first)
    def _():
        pad_scr[0:8, :] = jnp.zeros((8, conv_width), F32)

    @pl.when(jnp.logical_not(first))
    def _():
        pad_scr[0:8, :] = pad_scr[tm:tm + 8, :]

    for part in range(conv_width // GDN_WIDTH):
        cols = slice(part * GDN_WIDTH, (part + 1) * GDN_WIDTH)
        d = jnp.dot(hb, wm_ref[:, half + part * GDN_WIDTH:half + (part + 1) * GDN_WIDTH],
                    preferred_element_type=F32)
        pad_scr[8:, cols] = d
        acc = d * cw_ref[GDN_CONV - 1:GDN_CONV, cols]
        for j in range(GDN_CONV - 1):
            shift = GDN_CONV - 1 - j
            acc = acc + pad_scr[8 - shift:8 - shift + tm, cols] * cw_ref[j:j + 1, cols]
        act = _silu(acc)
        for hh in range(GDN_HEADS):
            head_cols = slice(hh * GDN_HEAD_DIM, (hh + 1) * GDN_HEAD_DIM)
            out_cols = slice(part * GDN_WIDTH + hh * GDN_HEAD_DIM, part * GDN_WIDTH + (hh + 1) * GDN_HEAD_DIM)
            if part == 0:
                gdn_ref[:, out_cols] = _l2_normalize(act[:, head_cols]) * (GDN_HEAD_DIM ** -0.5)
            elif part == 1:
                gdn_ref[:, out_cols] = _l2_normalize(act[:, head_cols])
            else:
                gdn_ref[:, out_cols] = act[:, head_cols]
        moba_ref[:, cols] = jnp.dot(hb, wm_ref[:, cols], preferred_element_type=F32)
    z = jnp.dot(hb, wm_ref[:, half + conv_width:], preferred_element_type=F32)
    gdn_ref[:, conv_width:] = _silu(z)
    ba = jnp.dot(hb, wt_ref[...], preferred_element_type=F32)
    moba_ref[:, conv_width:] = jnp.dot(hb, wm_ref[:, conv_width:half], preferred_element_type=F32)

    lane = lax.broadcasted_iota(jnp.int32, (1, LANES), 1)
    pre = ba + dtb_ref[...]
    softplus = jnp.maximum(pre, 0.0) + jnp.log(1.0 + jnp.exp(-jnp.abs(pre)))
    gate_ref[...] = jnp.where(lane < GDN_HEADS, 1.0 / (1.0 + jnp.exp(-ba)),
                              -jnp.exp(alog_ref[...]) * softplus)


def _inproj(x2, norm_w, w_main, w_tail, conv_w, a_log_lanes, dt_bias_lanes, tm, seq):
    n = x2.shape[0]
    half = w_main.shape[1] // 2
    assert seq % tm == 0 and tm % 8 == 0
    return pl.pallas_call(
        functools.partial(_inproj_kernel, seq // tm),
        grid=(n // tm,),
        in_specs=[
            pl.BlockSpec((tm, D_MODEL), lambda i: (i, 0)),
            pl.BlockSpec((1, D_MODEL), lambda i: (0, 0)),
            pl.BlockSpec(w_main.shape, lambda i: (0, 0)),
            pl.BlockSpec(w_tail.shape, lambda i: (0, 0)),
            pl.BlockSpec(conv_w.shape, lambda i: (0, 0)),
            pl.BlockSpec((1, LANES), lambda i: (0, 0)),
            pl.BlockSpec((1, LANES), lambda i: (0, 0)),
        ],
        out_specs=[
            pl.BlockSpec((tm, half), lambda i: (i, 0)),
            pl.BlockSpec((tm, half), lambda i: (i, 0)),
            pl.BlockSpec((tm, LANES), lambda i: (i, 0)),
        ],
        out_shape=[
            jax.ShapeDtypeStruct((n, half), F32),
            jax.ShapeDtypeStruct((n, half), F32),
            jax.ShapeDtypeStruct((n, LANES), F32),
        ],
        scratch_shapes=[pltpu.VMEM((tm + 8, conv_w.shape[1]), F32)],
        compiler_params=pltpu.CompilerParams(
            dimension_semantics=("arbitrary",), vmem_limit_bytes=VMEM_LIMIT_BYTES),
        name="inproj",
    )(x2, norm_w, w_main, w_tail, conv_w, a_log_lanes, dt_bias_lanes)


FEAT_POS_HI = MOBA_HEADS
FEAT_POS_LO = MOBA_HEADS + 1
FEAT_ONE_HI = MOBA_HEADS + 2
FEAT_ONE_LO = MOBA_HEADS + 3


def _moba_features(seq):
    blk = MOBA_BLOCK
    pos = np.arange(seq)
    kfeat = np.zeros((2, seq, LANES), np.float32)
    qfeat = np.zeros((2, 2, blk, LANES), np.float32)
    for h in range(2):
        base = MOBA_HEAD_DIM * (1 - h)
        kfeat[h, pos, base + pos // blk] = 1.0
        kfeat[h, :, base + FEAT_POS_HI] = blk * (pos // blk)
        kfeat[h, :, base + FEAT_POS_LO] = pos % blk
        kfeat[h, :, base + FEAT_ONE_HI] = 1.0
        kfeat[h, :, base + FEAT_ONE_LO] = 1.0
        qfeat[h, 0, :, base + FEAT_POS_HI] = 1.0
        qfeat[h, 0, :, base + FEAT_POS_LO] = 1.0
        qfeat[h, 0, :, base + FEAT_ONE_LO] = -np.arange(blk)
        qfeat[h, 1, :, base + FEAT_ONE_HI] = -blk
    return jnp.asarray(kfeat), jnp.asarray(qfeat)


def _moba_kernel(slopes_ref, q_ref, k_ref, v_ref, z_ref, kfeat_ref, qfeat_ref, o_ref,
                 kaug_scr, logit_scr):
    seq = q_ref.shape[1]
    blk = MOBA_BLOCK
    nb = seq // blk
    pair = pl.program_id(1)
    lane = lax.broadcasted_iota(jnp.int32, (1, LANES), 1)
    head_masks = [(lane >= MOBA_HEAD_DIM * h) & (lane < MOBA_HEAD_DIM * (h + 1)) for h in range(2)]
    feat_lane = [lane - MOBA_HEAD_DIM * (1 - h) for h in range(2)]
    slopes = [slopes_ref[pair * 2 + h] for h in range(2)]

    k_all = k_ref[0]
    vb16 = v_ref[0].astype(BF16)
    tile_row = lax.broadcasted_iota(jnp.int32, (LANES, LANES), 0)
    kbar = [jnp.zeros((LANES, LANES), F32) for _ in range(2)]
    for n in range(nb):
        mean_n = jnp.mean(k_all[n * blk:(n + 1) * blk], axis=0, keepdims=True)
        kbar = [jnp.where(tile_row == MOBA_HEAD_DIM * (1 - h) + n, mean_n, kbar[h]) for h in range(2)]
    for h in range(2):
        kaug_scr[h] = jnp.where(head_masks[h], k_all, kfeat_ref[h]).astype(BF16)
    row = lax.broadcasted_iota(jnp.int32, (blk, blk), 0)
    col = lax.broadcasted_iota(jnp.int32, (blk, blk), 1)
    causal = row >= col

    for i in range(nb):
        rows = slice(i * blk, (i + 1) * blk)
        q_blk = q_ref[0, rows, :]
        q_aug = []
        for h in range(2):
            qh = jnp.where(head_masks[h], q_blk, 0.0)
            feat = slopes[h] * (qfeat_ref[h, 0] + float(i) * qfeat_ref[h, 1])
            if i > MOBA_TOPK:
                fl = feat_lane[h]
                gate = lax.dot_general(qh, kbar[h], NT_DIMS, precision=lax.Precision.HIGHEST,
                                       preferred_element_type=F32)
                cnt = jnp.zeros((blk, LANES), F32)
                for s in range(1, i):
                    lower = pltpu.roll(gate, s, axis=1)
                    upper = pltpu.roll(gate, LANES - s, axis=1)
                    cnt = cnt + jnp.where((lower >= gate) & (fl >= s), 1.0, 0.0)
                    cnt = cnt + jnp.where((upper > gate) & (fl + s < i), 1.0, 0.0)
                dropped = (cnt >= MOBA_TOPK) & (fl >= 0) & (fl < i)
                feat = jnp.where(dropped, NEG_INF, feat)
            q_aug.append(jnp.where(head_masks[h], q_blk * (MOBA_HEAD_DIM ** -0.5), feat).astype(BF16))

        m = [jnp.full((blk, 1), NEG_INF, F32) for _ in range(2)]
        for n in range(i + 1):
            for h in range(2):
                tile = lax.dot_general(q_aug[h], kaug_scr[h, n * blk:(n + 1) * blk, :], NT_DIMS,
                                       preferred_element_type=F32)
                if n == i:
                    tile = jnp.where(causal, tile, NEG_INF)
                logit_scr[h, n] = tile
                m[h] = jnp.maximum(m[h], jnp.max(tile, axis=1, keepdims=True))
        l = [jnp.zeros((blk, 1), F32) for _ in range(2)]
        acc = jnp.zeros((2 * blk, LANES), F32)
        for n in range(i + 1):
            p = [jnp.exp(logit_scr[h, n] - m[h]) for h in range(2)]
            l = [l[h] + jnp.sum(p[h], axis=1, keepdims=True) for h in range(2)]
            acc = acc + jnp.dot(jnp.concatenate(p, axis=0).astype(BF16), vb16[n * blk:(n + 1) * blk],
                                preferred_element_type=F32)
        out = jnp.where(head_masks[0], acc[:blk] / l[0], acc[blk:] / l[1])
        o_ref[0, rows, :] = out * _silu(z_ref[0, rows, :])


def _moba(slopes, proj_a, batch, seq):
    npair = MOBA_WIDTH // LANES
    proj_a = proj_a.reshape(batch, seq, 4 * MOBA_WIDTH)
    nb = seq // MOBA_BLOCK
    assert nb <= MOBA_HEADS, "block-indicator feature lanes"
    kfeat, qfeat = _moba_features(seq)

    def col_block(base):
        return pl.BlockSpec((1, seq, LANES), lambda b, p, *_: (b, 0, base + p))

    return pl.pallas_call(
        _moba_kernel,
        grid_spec=pltpu.PrefetchScalarGridSpec(
            num_scalar_prefetch=1,
            grid=(batch, npair),
            in_specs=[col_block(0), col_block(npair), col_block(2 * npair), col_block(3 * npair),
                      pl.BlockSpec(kfeat.shape, lambda b, p, *_: (0, 0, 0)),
                      pl.BlockSpec(qfeat.shape, lambda b, p, *_: (0, 0, 0, 0))],
            out_specs=pl.BlockSpec((1, seq, LANES), lambda b, p, *_: (b, 0, p)),
            scratch_shapes=[pltpu.VMEM((2, seq, LANES), BF16),
                            pltpu.VMEM((2, nb, MOBA_BLOCK, MOBA_BLOCK), F32)],
        ),
        out_shape=jax.ShapeDtypeStruct((batch, seq, MOBA_WIDTH), F32),
        compiler_params=pltpu.CompilerParams(
            dimension_semantics=("arbitrary", "arbitrary"), vmem_limit_bytes=VMEM_LIMIT_BYTES),
        name="moba",
    )(slopes, proj_a, proj_a, proj_a, proj_a, kfeat, qfeat)


def _each(fn, *lists):
    return [fn(*args) for args in zip(*lists)]


def _unit_lower_inverse(a_list, row, col, chunk):
    def same_block(shift):
        return jnp.right_shift(row, shift) == jnp.right_shift(col, shift)

    eye = jnp.where(row == col, 1.0, 0.0)
    inner = same_block(4)
    d = _each(lambda a: jnp.where(inner, a, 0.0), a_list)
    x = _each(lambda dd: eye - dd, d)
    p = _each(_dot, d, d)
    for step in range(3):
        x = _each(lambda xx, pp: xx + _dot(xx, pp), x, p)
        if step < 2:
            p = _each(_dot, p, p)
    shift = 5
    while (1 << shift) <= chunk:
        outer = same_block(shift)
        coupling_mask = outer & jnp.logical_not(inner)
        lx = _each(lambda a, xx: _dot(jnp.where(coupling_mask, a, 0.0), xx), a_list, x)
        x = _each(lambda xx, ll: xx - _dot(xx, ll), x, lx)
        inner = outer
        shift += 1
    return x


def _gdn_kernel(q_ref, k_ref, v_ref, zgate_ref, ba_ref, gnw_ref, o_ref,
                gate_scr, u_scr, lhs_scr, n_scr, egl_scr, intra_scr):
    seq = q_ref.shape[1]
    sup = GDN_SUPER
    chunk = GDN_CHUNK
    per_sup = sup // chunk
    nsup = seq // sup
    hps = GDN_HEADS_PER_STEP
    lhs_rows = 2 * chunk + GDN_HEAD_DIM
    group = pl.program_id(1)
    lane = lax.broadcasted_iota(jnp.int32, (1, LANES), 1)
    row = lax.broadcasted_iota(jnp.int32, (sup, sup), 0)
    col = lax.broadcasted_iota(jnp.int32, (sup, sup), 1)
    chunk_shift = chunk.bit_length() - 1
    same_chunk = jnp.right_shift(row, chunk_shift) == jnp.right_shift(col, chunk_shift)
    causal = same_chunk & (row >= col)
    strict = same_chunk & (row > col)

    @pl.when(group == 0)
    def _():
        tri = jnp.where(causal, 1.0, 0.0)
        for s in range(nsup):
            rows = slice(s * sup, (s + 1) * sup)
            gates = ba_ref[0, rows, :]
            gc = jnp.dot(tri, gates, precision=lax.Precision.HIGHEST, preferred_element_type=F32)
            gate_scr[rows, :] = jnp.where(lane < GDN_HEADS, gates, gc)

    tiles_per_step = GDN_PREP_GROUP // hps

    def prepare(step, carry):
        loaded = []
        for t in range(tiles_per_step):
            rows = pl.ds(pl.multiple_of((step * tiles_per_step + t) * sup, sup), sup)
            for j in range(hps):
                cols = slice(j * LANES, (j + 1) * LANES)
                loaded.append((group * hps + j, gate_scr[rows, :], q_ref[0, rows, cols], k_ref[0, rows, cols],
                               v_ref[0, rows, cols]))
        results = prepare_tiles(*zip(*loaded))
        for t in range(tiles_per_step):
            for j in range(hps):
                store_tile(j, step * tiles_per_step + t, *[r[t * hps + j] for r in results])
        return carry

    def decay_terms(head, gates):
        beta = jnp.sum(jnp.where(lane == head, gates, 0.0), axis=1, keepdims=True)
        gc = jnp.sum(jnp.where(lane == GDN_HEADS + head, gates, 0.0), axis=1, keepdims=True)
        gc_b = jnp.broadcast_to(gc, (sup, LANES))
        gc_row = jnp.concatenate([gc_b[0:LANES].T, gc_b[LANES:].T], axis=1)[0:1, :]
        decay = jnp.exp(jnp.where(causal, gc - gc_row, NEG_INF))
        gl_b = jnp.concatenate(
            [jnp.broadcast_to(gc_b[(c + 1) * chunk - 1:(c + 1) * chunk, :], (chunk, LANES))
             for c in range(per_sup)], axis=0)
        return beta, decay, jnp.exp(gc), jnp.exp(gl_b - gc_b), jnp.exp(gl_b)

    def state_terms(k_dec, w, u, q_dec, egl):
        lhs, ns, egls = [], [], []
        for c in range(per_sup):
            cr = slice(c * chunk, (c + 1) * chunk)
            pn = lax.dot_general(k_dec[cr].astype(BF16),
                                 jnp.concatenate([w[cr], u[cr]], axis=1).astype(BF16),
                                 TN_DIMS, preferred_element_type=F32)
            lhs.append(jnp.concatenate([w[cr], q_dec[cr], pn[:, :GDN_HEAD_DIM]], axis=0).astype(BF16))
            ns.append(pn[:, GDN_HEAD_DIM:])
            egls.append(egl[c * chunk:c * chunk + 8])
        return lhs, ns, egls

    def prepare_tiles(heads, gates, qb, kb, vb):
        beta, decay, egc, e_tail, egl = zip(*_each(decay_terms, heads, gates))
        k_beta = _each(lambda k, b: k * b, kb, beta)
        a = _each(lambda kbt, k, dc: jnp.where(strict, _dot_nt(kbt, k) * dc, 0.0), k_beta, kb, decay)
        t_inv = _unit_lower_inverse(a, row, col, chunk)
        wu = _each(lambda t, kbt, e, v, b: _dot(t, jnp.concatenate([kbt * e, v * b], axis=1)),
                   t_inv, k_beta, egc, vb, beta)
        w = [x[:, :GDN_HEAD_DIM] for x in wu]
        u = [x[:, GDN_HEAD_DIM:] for x in wu]
        intra = _each(lambda q, k, dc: jnp.where(causal, _dot_nt(q, k) * dc, 0.0).astype(BF16),
                      qb, kb, decay)
        q_dec = _each(lambda q, e: q * e, qb, egc)
        k_dec = _each(lambda k, e: k * e, kb, e_tail)
        lhs, ns, egls = zip(*_each(state_terms, k_dec, w, u, q_dec, egl))
        return u, intra, lhs, ns, egls

    def chunk_slot(s, c, size):
        return pl.ds(pl.multiple_of((s * per_sup + c) * size, size), size)

    def store_tile(j, s, u, intra, lhs, ns, egls):
        rows = pl.ds(pl.multiple_of(s * sup, sup), sup)
        u_scr[j, rows, :] = u
        intra_scr[j, rows, :] = intra
        for c in range(per_sup):
            lhs_scr[j, chunk_slot(s, c, lhs_rows), :] = lhs[c]
            n_scr[j, chunk_slot(s, c, GDN_HEAD_DIM), :] = ns[c]
            egl_scr[j, chunk_slot(s, c, 8), :] = egls[c]

    lax.fori_loop(0, nsup // tiles_per_step, prepare, 0)

    gnw = gnw_ref[...]

    def scan(s, states):
        r0 = pl.multiple_of(s * sup, sup)
        states = list(states)
        v_parts = [[] for _ in range(hps)]
        for c in range(per_sup):
            rows = pl.ds(r0 + c * chunk, chunk)
            r = [jnp.dot(lhs_scr[j, chunk_slot(s, c, lhs_rows), :], states[j].astype(BF16),
                         preferred_element_type=F32) for j in range(hps)]
            outs = []
            for j in range(hps):
                v_parts[j].append(u_scr[j, rows, :] - r[j][:chunk])
                v_sup = jnp.concatenate(
                    v_parts[j] + [jnp.zeros(((per_sup - 1 - c) * chunk, LANES), F32)] * (c < per_sup - 1),
                    axis=0)
                o = r[j][chunk:2 * chunk] + jnp.dot(intra_scr[j, rows, :], v_sup.astype(BF16),
                                                    preferred_element_type=F32)
                egl = jnp.broadcast_to(egl_scr[j, chunk_slot(s, c, 8), :][0:1, :], (GDN_HEAD_DIM, LANES))
                states[j] = states[j] * egl - r[j][2 * chunk:] + n_scr[j, chunk_slot(s, c, GDN_HEAD_DIM), :]
                outs.append(o * lax.rsqrt(jnp.mean(o * o, axis=-1, keepdims=True) + RMS_EPS) * gnw)
            o_ref[0, rows, :] = jnp.concatenate(outs, axis=1) * zgate_ref[0, rows, :]
        return tuple(states)

    lax.fori_loop(0, nsup, scan, tuple(jnp.zeros((GDN_HEAD_DIM, GDN_HEAD_DIM), F32) for _ in range(hps)))


def _gdn(act_d, gates, gdn_norm_w, batch, seq):
    hps = GDN_HEADS_PER_STEP
    ngroup = GDN_HEADS // hps
    width = hps * GDN_HEAD_DIM
    assert GDN_PREP_GROUP % hps == 0 and (seq // GDN_SUPER) % (GDN_PREP_GROUP // hps) == 0
    act_d = act_d.reshape(batch, seq, 4 * GDN_WIDTH)
    gates = gates.reshape(batch, seq, LANES)

    def col_block(base):
        return pl.BlockSpec((1, seq, width), lambda b, g: (b, 0, base + g))

    def per_head(rows, lanes, dtype):
        return pltpu.VMEM((hps, rows, lanes), dtype)

    nchunk = seq // GDN_CHUNK
    scratch = [
        pltpu.VMEM((seq, LANES), F32),
        per_head(seq, LANES, F32),
        per_head(nchunk * (2 * GDN_CHUNK + GDN_HEAD_DIM), LANES, BF16),
        per_head(nchunk * GDN_HEAD_DIM, LANES, F32),
        per_head(nchunk * 8, LANES, F32),
        per_head(seq, GDN_SUPER, BF16),
    ]
    return pl.pallas_call(
        _gdn_kernel,
        grid=(batch, ngroup),
        in_specs=[col_block(0), col_block(ngroup), col_block(2 * ngroup), col_block(3 * ngroup),
                  pl.BlockSpec((1, seq, LANES), lambda b, g: (b, 0, 0)),
                  pl.BlockSpec((1, LANES), lambda b, g: (0, 0))],
        out_specs=pl.BlockSpec((1, seq, width), lambda b, g: (b, 0, g)),
        scratch_shapes=scratch,
        out_shape=jax.ShapeDtypeStruct((batch, seq, GDN_WIDTH), F32),
        compiler_params=pltpu.CompilerParams(
            dimension_semantics=("arbitrary", "arbitrary"), vmem_limit_bytes=VMEM_LIMIT_BYTES),
        name="gdn",
    )(act_d, act_d, act_d, act_d, gates, gdn_norm_w)


def _outproj_kernel(x_ref, ma_ref, md_ref, wa_ref, wd_ref, nw_ref, o_ref):
    y = (jnp.dot(ma_ref[...].astype(BF16), wa_ref[...], preferred_element_type=F32)
         + jnp.dot(md_ref[...].astype(BF16), wd_ref[...], preferred_element_type=F32))
    y = y * lax.rsqrt(jnp.mean(y * y, axis=-1, keepdims=True) + RMS_EPS) * nw_ref[...]
    o_ref[...] = x_ref[...] + y


def _outproj(x2, mix_a, mix_d, w_a, w_d, norm_w, tm):
    n = x2.shape[0]
    return pl.pallas_call(
        _outproj_kernel,
        grid=(n // tm,),
        in_specs=[
            pl.BlockSpec((tm, D_MODEL), lambda i: (i, 0)),
            pl.BlockSpec((tm, MOBA_WIDTH), lambda i: (i, 0)),
            pl.BlockSpec((tm, GDN_WIDTH), lambda i: (i, 0)),
            pl.BlockSpec(w_a.shape, lambda i: (0, 0)),
            pl.BlockSpec(w_d.shape, lambda i: (0, 0)),
            pl.BlockSpec((1, D_MODEL), lambda i: (0, 0)),
        ],
        out_specs=pl.BlockSpec((tm, D_MODEL), lambda i: (i, 0)),
        out_shape=jax.ShapeDtypeStruct((n, D_MODEL), F32),
        compiler_params=pltpu.CompilerParams(
            dimension_semantics=("arbitrary",), vmem_limit_bytes=VMEM_LIMIT_BYTES),
        name="outproj",
    )(x2, mix_a, mix_d, w_a, w_d, norm_w)


def _alibi_slopes():
    start = 2.0 ** (-8.0 / MOBA_HEADS)
    slopes = np.asarray(start ** np.arange(1, MOBA_HEADS + 1), dtype=np.float32)
    return jnp.asarray(slopes)


def _layer(x, norm_pre_w, w_in, conv_w, a_log, dt_bias, gdn_norm_w, w_out, norm_post_w):
    batch, seq, _ = x.shape
    assert seq % MOBA_BLOCK == 0 and seq % GDN_SUPER == 0
    n = batch * seq
    tm = 512
    assert n % tm == 0
    x2 = x.reshape(n, D_MODEL)
    n_main = 4 * MOBA_WIDTH + 4 * GDN_WIDTH
    w_main = w_in[:, :n_main].astype(BF16)
    w_tail = jnp.pad(w_in[:, n_main:], ((0, 0), (0, LANES - 2 * GDN_HEADS))).astype(BF16)
    to_decay_lanes = (GDN_HEADS, LANES - 2 * GDN_HEADS)
    a_log_lanes = jnp.pad(a_log.astype(F32), to_decay_lanes).reshape(1, LANES)
    dt_bias_lanes = jnp.pad(dt_bias.astype(F32), to_decay_lanes).reshape(1, LANES)
    proj_a, act_d, gates = _inproj(x2, norm_pre_w.reshape(1, D_MODEL), w_main, w_tail, conv_w.astype(F32),
                                   a_log_lanes, dt_bias_lanes, tm, seq)
    mix_a = _moba(_alibi_slopes(), proj_a, batch, seq)
    mix_d = _gdn(act_d, gates, gdn_norm_w.reshape(1, GDN_HEAD_DIM), batch, seq)
    w_o = w_out.astype(BF16)
    out = _outproj(x2, mix_a.reshape(n, MOBA_WIDTH), mix_d.reshape(n, GDN_WIDTH),
                   w_o[:MOBA_WIDTH], w_o[MOBA_WIDTH:], norm_post_w.reshape(1, D_MODEL), tm)
    return out.reshape(batch, seq, D_MODEL)


def kernel(x, norm_pre_w, w_in, conv_w, a_log, dt_bias, gdn_norm_w, w_out, norm_post_w):
    for layer in range(norm_pre_w.shape[0]):
        x = _layer(x, norm_pre_w[layer], w_in[layer], conv_w[layer], a_log[layer], dt_bias[layer],
                   gdn_norm_w[layer], w_out[layer], norm_post_w[layer])
    return x
```

```python
import functools

import numpy as np
import jax
import jax.numpy as jnp
from jax import lax
from jax.experimental import pallas as pl
from jax.experimental.pallas import tpu as pltpu

F32 = jnp.float32
BF16 = jnp.bfloat16

D_MODEL = 1024
MOBA_WIDTH = 512
MOBA_HEAD_DIM = 64
MOBA_HEADS = MOBA_WIDTH // MOBA_HEAD_DIM
MOBA_BLOCK = 256
MOBA_TOPK = 3
GDN_WIDTH = 512
GDN_HEAD_DIM = 128
GDN_HEADS = GDN_WIDTH // GDN_HEAD_DIM
GDN_CONV = 4
GDN_CHUNK = 128
GDN_SUPER = 256
GDN_HEADS_PER_STEP = 2
GDN_PREP_GROUP = 4
RMS_EPS = 1e-6
NEG_INF = -1e30

LANES = 128
VMEM_LIMIT_BYTES = 48 * 1024 * 1024

NT_DIMS = (((1,), (1,)), ((), ()))
TN_DIMS = (((0,), (0,)), ((), ()))


def _dot(a, b):
    return jnp.dot(a.astype(BF16), b.astype(BF16), preferred_element_type=F32)


def _dot_nt(a, b):
    return lax.dot_general(a.astype(BF16), b.astype(BF16), NT_DIMS, preferred_element_type=F32)


def _silu(x):
    return x / (1.0 + jnp.exp(-x))


def _l2_normalize(x):
    return x * lax.rsqrt(jnp.sum(x * x, axis=-1, keepdims=True) + RMS_EPS)


def _inproj_kernel(tiles_per_seq, x_ref, nw_ref, wm_ref, wt_ref, cw_ref, alog_ref, dtb_ref,
                   moba_ref, gdn_ref, gate_ref, pad_scr):
    tm = x_ref.shape[0]
    x = x_ref[...]
    h = x * lax.rsqrt(jnp.mean(x * x, axis=-1, keepdims=True) + RMS_EPS) * nw_ref[...]
    hb = h.astype(BF16)
    half = moba_ref.shape[-1]

    conv_width = pad_scr.shape[1]
    first = (pl.program_id(0) % tiles_per_seq) == 0

    @pl.when(first)
    def _():
        pad_scr[0:8, :] = jnp.zeros((8, conv_width), F32)

    @pl.when(jnp.logical_not(first))
    def _():
        pad_scr[0:8, :] = pad_scr[tm:tm + 8, :]

    for part in range(conv_width // GDN_WIDTH):
        cols = slice(part * GDN_WIDTH, (part + 1) * GDN_WIDTH)
        d = jnp.dot(hb, wm_ref[:, half + part * GDN_WIDTH:half + (part + 1) * GDN_WIDTH],
                    preferred_element_type=F32)
        pad_scr[8:, cols] = d
        acc = d * cw_ref[GDN_CONV - 1:GDN_CONV, cols]
        for j in range(GDN_CONV - 1):
            shift = GDN_CONV - 1 - j
            acc = acc + pad_scr[8 - shift:8 - shift + tm, cols] * cw_ref[j:j + 1, cols]
        act = _silu(acc)
        for hh in range(GDN_HEADS):
            head_cols = slice(hh * GDN_HEAD_DIM, (hh + 1) * GDN_HEAD_DIM)
            out_cols = slice(part * GDN_WIDTH + hh * GDN_HEAD_DIM, part * GDN_WIDTH + (hh + 1) * GDN_HEAD_DIM)
            if part == 0:
                gdn_ref[:, out_cols] = _l2_normalize(act[:, head_cols]) * (GDN_HEAD_DIM ** -0.5)
            elif part == 1:
                gdn_ref[:, out_cols] = _l2_normalize(act[:, head_cols])
            else:
                gdn_ref[:, out_cols] = act[:, head_cols]
        moba_ref[:, cols] = jnp.dot(hb, wm_ref[:, cols], preferred_element_type=F32)
    z = jnp.dot(hb, wm_ref[:, half + conv_width:], preferred_element_type=F32)
    gdn_ref[:, conv_width:] = _silu(z)
    ba = jnp.dot(hb, wt_ref[...], preferred_element_type=F32)
    moba_ref[:, conv_width:] = jnp.dot(hb, wm_ref[:, conv_width:half], preferred_element_type=F32)

    lane = lax.broadcasted_iota(jnp.int32, (1, LANES), 1)
    pre = ba + dtb_ref[...]
    softplus = jnp.maximum(pre, 0.0) + jnp.log(1.0 + jnp.exp(-jnp.abs(pre)))
    gate_ref[...] = jnp.where(lane < GDN_HEADS, 1.0 / (1.0 + jnp.exp(-ba)),
                              -jnp.exp(alog_ref[...]) * softplus)


def _inproj(x2, norm_w, w_main, w_tail, conv_w, a_log_lanes, dt_bias_lanes, tm, seq):
    n = x2.shape[0]
    half = w_main.shape[1] // 2
    assert seq % tm == 0 and tm % 8 == 0
    return pl.pallas_call(
        functools.partial(_inproj_kernel, seq // tm),
        grid=(n // tm,),
        in_specs=[
            pl.BlockSpec((tm, D_MODEL), lambda i: (i, 0)),
            pl.BlockSpec((1, D_MODEL), lambda i: (0, 0)),
            pl.BlockSpec(w_main.shape, lambda i: (0, 0)),
            pl.BlockSpec(w_tail.shape, lambda i: (0, 0)),
            pl.BlockSpec(conv_w.shape, lambda i: (0, 0)),
            pl.BlockSpec((1, LANES), lambda i: (0, 0)),
            pl.BlockSpec((1, LANES), lambda i: (0, 0)),
        ],
        out_specs=[
            pl.BlockSpec((tm, half), lambda i: (i, 0)),
            pl.BlockSpec((tm, half), lambda i: (i, 0)),
            pl.BlockSpec((tm, LANES), lambda i: (i, 0)),
        ],
        out_shape=[
            jax.ShapeDtypeStruct((n, half), F32),
            jax.ShapeDtypeStruct((n, half), F32),
            jax.ShapeDtypeStruct((n, LANES), F32),
        ],
        scratch_shapes=[pltpu.VMEM((tm + 8, conv_w.shape[1]), F32)],
        compiler_params=pltpu.CompilerParams(
            dimension_semantics=("arbitrary",), vmem_limit_bytes=VMEM_LIMIT_BYTES),
        name="inproj",
    )(x2, norm_w, w_main, w_tail, conv_w, a_log_lanes, dt_bias_lanes)


MOBA_MAX_BLOCKS = 8
FEAT_POS_HI = MOBA_MAX_BLOCKS
FEAT_POS_LO = MOBA_MAX_BLOCKS + 1
FEAT_ONE_HI = MOBA_MAX_BLOCKS + 2
FEAT_ONE_LO = MOBA_MAX_BLOCKS + 3


def _moba_features(seq):
    blk = MOBA_BLOCK
    pos = np.arange(seq)
    kfeat = np.zeros((2, seq, LANES), np.float32)
    qfeat = np.zeros((2, seq, LANES), np.float32)
    for h in range(2):
        base = MOBA_HEAD_DIM * (1 - h)
        kfeat[h, pos, base + pos // blk] = 1.0
        kfeat[h, :, base + FEAT_POS_HI] = blk * (pos // blk)
        kfeat[h, :, base + FEAT_POS_LO] = pos % blk
        kfeat[h, :, base + FEAT_ONE_HI] = 1.0
        kfeat[h, :, base + FEAT_ONE_LO] = 1.0
        qfeat[h, :, base + FEAT_POS_HI] = 1.0
        qfeat[h, :, base + FEAT_POS_LO] = 1.0
        qfeat[h, :, base + FEAT_ONE_HI] = -blk * (pos // blk)
        qfeat[h, :, base + FEAT_ONE_LO] = -(pos % blk)
    return jnp.asarray(kfeat), jnp.asarray(qfeat)


def _moba_kernel(slopes_ref, q_ref, k_ref, v_ref, z_ref, kfeat_ref, qfeat_ref, o_ref,
                 kaug_scr, qaug_scr, vaug_scr, logit_scr):
    seq = q_ref.shape[1]
    blk = MOBA_BLOCK
    nb = seq // blk
    pair = pl.program_id(1)
    lane = lax.broadcasted_iota(jnp.int32, (1, LANES), 1)
    head_masks = [(lane >= MOBA_HEAD_DIM * h) & (lane < MOBA_HEAD_DIM * (h + 1)) for h in range(2)]
    feat_lane = [lane - MOBA_HEAD_DIM * (1 - h) for h in range(2)]
    slopes = [slopes_ref[pair * 2 + h] for h in range(2)]

    k_all = k_ref[0]
    vaug_scr[:, :LANES] = v_ref[0].astype(BF16)
    vaug_scr[:, LANES:] = jnp.ones((seq, LANES), BF16)
    row8 = lax.broadcasted_iota(jnp.int32, (MOBA_MAX_BLOCKS, LANES), 0)
    kbar = jnp.zeros((MOBA_MAX_BLOCKS, LANES), F32)
    for n in range(nb):
        kbar = jnp.where(row8 == n, jnp.mean(k_all[n * blk:(n + 1) * blk], axis=0, keepdims=True), kbar)
    to_feat_lanes = [jnp.where(feat_lane[h] == row8, 1.0, 0.0) for h in range(2)]
    for h in range(2):
        kaug_scr[h] = jnp.where(head_masks[h], k_all, kfeat_ref[h]).astype(BF16)
    row = lax.broadcasted_iota(jnp.int32, (blk, blk), 0)
    col = lax.broadcasted_iota(jnp.int32, (blk, blk), 1)
    causal = row >= col

    q_all = q_ref[0]
    ranked = (MOBA_TOPK + 1) * blk
    block_id = lax.broadcasted_iota(jnp.int32, (MOBA_MAX_BLOCKS, seq - ranked), 0)
    own = jnp.right_shift(lax.broadcasted_iota(jnp.int32, (MOBA_MAX_BLOCKS, seq - ranked), 1) + ranked,
                          blk.bit_length() - 1)
    gates = [lax.dot_general(kbar, jnp.where(head_masks[h], q_all[ranked:], 0.0), NT_DIMS,
                             precision=lax.Precision.HIGHEST, preferred_element_type=F32)
             for h in range(2)]
    cnt = [jnp.zeros((MOBA_MAX_BLOCKS, seq - ranked), F32) for _ in range(2)]
    for other in range(nb - 1):
        for h in range(2):
            g_other = gates[h][other:other + 1, :]
            before = (g_other > gates[h]) | ((g_other == gates[h]) & (block_id > other))
            cnt[h] = cnt[h] + jnp.where(before & (own > other), 1.0, 0.0)
    for h in range(2):
        dropped = jnp.where((cnt[h] >= MOBA_TOPK) & (block_id < own), NEG_INF, 0.0)
        feat = slopes[h] * qfeat_ref[h]
        scaled = q_all * (MOBA_HEAD_DIM ** -0.5)
        qaug_scr[h, :ranked, :] = jnp.where(head_masks[h], scaled[:ranked], feat[:ranked]).astype(BF16)
        sel_feat = lax.dot_general(dropped, to_feat_lanes[h], TN_DIMS, preferred_element_type=F32)
        qaug_scr[h, ranked:, :] = jnp.where(head_masks[h], scaled[ranked:],
                                            feat[ranked:] + sel_feat).astype(BF16)

    for i in range(nb):
        rows = slice(i * blk, (i + 1) * blk)
        q_aug = [qaug_scr[h, rows, :] for h in range(2)]

        mx = [None, None]
        for n in range(i + 1):
            for h in range(2):
                tile = lax.dot_general(q_aug[h], kaug_scr[h, n * blk:(n + 1) * blk, :], NT_DIMS,
                                       preferred_element_type=F32)
                if n == i:
                    tile = jnp.where(causal, tile, NEG_INF)
                logit_scr[h, n] = tile
                halves = jnp.maximum(tile[:, :LANES], tile[:, LANES:])
                mx[h] = halves if mx[h] is None else jnp.maximum(mx[h], halves)
        m = [jnp.max(mx[h], axis=1, keepdims=True) for h in range(2)]
        acc = jnp.zeros((2 * blk, 2 * LANES), F32)
        for n in range(i + 1):
            p = [jnp.exp(logit_scr[h, n] - m[h]) for h in range(2)]
            acc = acc + jnp.dot(jnp.concatenate(p, axis=0).astype(BF16), vaug_scr[n * blk:(n + 1) * blk, :],
                                preferred_element_type=F32)
        normed = acc[:, :LANES] / acc[:, LANES:]
        out = jnp.where(head_masks[0], normed[:blk], normed[blk:])
        o_ref[0, rows, :] = out * _silu(z_ref[0, rows, :])


def _moba(slopes, proj_a, batch, seq):
    npair = MOBA_WIDTH // LANES
    proj_a = proj_a.reshape(batch, seq, 4 * MOBA_WIDTH)
    nb = seq // MOBA_BLOCK
    assert nb <= MOBA_MAX_BLOCKS, "block-indicator feature lanes"
    kfeat, qfeat = _moba_features(seq)

    def col_block(base):
        return pl.BlockSpec((1, seq, LANES), lambda b, p, *_: (b, 0, base + p))

    return pl.pallas_call(
        _moba_kernel,
        grid_spec=pltpu.PrefetchScalarGridSpec(
            num_scalar_prefetch=1,
            grid=(batch, npair),
            in_specs=[col_block(0), col_block(npair), col_block(2 * npair), col_block(3 * npair),
                      pl.BlockSpec(kfeat.shape, lambda b, p, *_: (0, 0, 0)),
                      pl.BlockSpec(qfeat.shape, lambda b, p, *_: (0, 0, 0))],
            out_specs=pl.BlockSpec((1, seq, LANES), lambda b, p, *_: (b, 0, p)),
            scratch_shapes=[pltpu.VMEM((2, seq, LANES), BF16),
                            pltpu.VMEM((2, seq, LANES), BF16),
                            pltpu.VMEM((seq, 2 * LANES), BF16),
                            pltpu.VMEM((2, nb, MOBA_BLOCK, MOBA_BLOCK), F32)],
        ),
        out_shape=jax.ShapeDtypeStruct((batch, seq, MOBA_WIDTH), F32),
        compiler_params=pltpu.CompilerParams(
            dimension_semantics=("arbitrary", "arbitrary"), vmem_limit_bytes=VMEM_LIMIT_BYTES),
        name="moba",
    )(slopes, proj_a, proj_a, proj_a, proj_a, kfeat, qfeat)


def _each(fn, *lists):
    return [fn(*args) for args in zip(*lists)]


def _unit_lower_inverse(a_list, row, col, chunk):
    def same_block(shift):
        return jnp.right_shift(row, shift) == jnp.right_shift(col, shift)

    eye = jnp.where(row == col, 1.0, 0.0)
    inner = same_block(4)
    d = _each(lambda a: jnp.where(inner, a, 0.0), a_list)
    x = _each(lambda dd: eye - dd, d)
    p = _each(_dot, d, d)
    for step in range(3):
        x = _each(lambda xx, pp: xx + _dot(xx, pp), x, p)
        if step < 2:
            p = _each(_dot, p, p)
    shift = 5
    while (1 << shift) <= chunk:
        outer = same_block(shift)
        coupling_mask = outer & jnp.logical_not(inner)
        lx = _each(lambda a, xx: _dot(jnp.where(coupling_mask, a, 0.0), xx), a_list, x)
        x = _each(lambda xx, ll: xx - _dot(xx, ll), x, lx)
        inner = outer
        shift += 1
    return x


def _gdn_kernel(q_ref, k_ref, v_ref, zgate_ref, ba_ref, gnw_ref, o_ref,
                gate_scr, u_scr, lhs_scr, n_scr, egl_scr, intra_scr):
    seq = q_ref.shape[1]
    sup = GDN_SUPER
    chunk = GDN_CHUNK
    per_sup = sup // chunk
    nsup = seq // sup
    hps = GDN_HEADS_PER_STEP
    lhs_rows = 2 * chunk + GDN_HEAD_DIM
    group = pl.program_id(1)
    lane = lax.broadcasted_iota(jnp.int32, (1, LANES), 1)
    row = lax.broadcasted_iota(jnp.int32, (sup, sup), 0)
    col = lax.broadcasted_iota(jnp.int32, (sup, sup), 1)
    chunk_shift = chunk.bit_length() - 1
    same_chunk = jnp.right_shift(row, chunk_shift) == jnp.right_shift(col, chunk_shift)
    causal = same_chunk & (row >= col)
    strict = same_chunk & (row > col)

    @pl.when(group == 0)
    def _():
        tri = jnp.where(causal, 1.0, 0.0)
        for s in range(nsup):
            rows = slice(s * sup, (s + 1) * sup)
            gates = ba_ref[0, rows, :]
            gc = jnp.dot(tri, gates, precision=lax.Precision.HIGHEST, preferred_element_type=F32)
            gate_scr[rows, :] = jnp.where(lane < GDN_HEADS, gates, gc)

    tiles_per_step = GDN_PREP_GROUP // hps

    def prepare(step, carry):
        loaded = []
        for t in range(tiles_per_step):
            rows = pl.ds(pl.multiple_of((step * tiles_per_step + t) * sup, sup), sup)
            for j in range(hps):
                cols = slice(j * LANES, (j + 1) * LANES)
                loaded.append((group * hps + j, gate_scr[rows, :], q_ref[0, rows, cols], k_ref[0, rows, cols],
                               v_ref[0, rows, cols]))
        results = prepare_tiles(*zip(*loaded))
        for t in range(tiles_per_step):
            for j in range(hps):
                store_tile(j, step * tiles_per_step + t, *[r[t * hps + j] for r in results])
        return carry

    def decay_terms(head, gates):
        beta = jnp.sum(jnp.where(lane == head, gates, 0.0), axis=1, keepdims=True)
        gc = jnp.sum(jnp.where(lane == GDN_HEADS + head, gates, 0.0), axis=1, keepdims=True)
        gc_b = jnp.broadcast_to(gc, (sup, LANES))
        gc_row = jnp.concatenate([gc_b[0:LANES].T, gc_b[LANES:].T], axis=1)[0:1, :]
        decay = jnp.exp(jnp.where(causal, gc - gc_row, NEG_INF))
        gl_b = jnp.concatenate(
            [jnp.broadcast_to(gc_b[(c + 1) * chunk - 1:(c + 1) * chunk, :], (chunk, LANES))
             for c in range(per_sup)], axis=0)
        return beta, decay, jnp.exp(gc), jnp.exp(gl_b - gc_b), jnp.exp(gl_b)

    def state_terms(k_dec, w, u, q_dec, egl):
        lhs, ns, egls = [], [], []
        for c in range(per_sup):
            cr = slice(c * chunk, (c + 1) * chunk)
            pn = lax.dot_general(k_dec[cr].astype(BF16),
                                 jnp.concatenate([w[cr], u[cr]], axis=1).astype(BF16),
                                 TN_DIMS, preferred_element_type=F32)
            lhs.append(jnp.concatenate([w[cr], q_dec[cr], pn[:, :GDN_HEAD_DIM]], axis=0).astype(BF16))
            ns.append(pn[:, GDN_HEAD_DIM:])
            egls.append(egl[c * chunk:c * chunk + 8])
        return lhs, ns, egls

    def prepare_tiles(heads, gates, qb, kb, vb):
        beta, decay, egc, e_tail, egl = zip(*_each(decay_terms, heads, gates))
        k_beta = _each(lambda k, b: k * b, kb, beta)
        a = _each(lambda kbt, k, dc: jnp.where(strict, _dot_nt(kbt, k) * dc, 0.0), k_beta, kb, decay)
        t_inv = _unit_lower_inverse(a, row, col, chunk)
        wu = _each(lambda t, kbt, e, v, b: _dot(t, jnp.concatenate([kbt * e, v * b], axis=1)),
                   t_inv, k_beta, egc, vb, beta)
        w = [x[:, :GDN_HEAD_DIM] for x in wu]
        u = [x[:, GDN_HEAD_DIM:] for x in wu]
        intra = _each(lambda q, k, dc: jnp.where(causal, _dot_nt(q, k) * dc, 0.0).astype(BF16),
                      qb, kb, decay)
        q_dec = _each(lambda q, e: q * e, qb, egc)
        k_dec = _each(lambda k, e: k * e, kb, e_tail)
        lhs, ns, egls = zip(*_each(state_terms, k_dec, w, u, q_dec, egl))
        return u, intra, lhs, ns, egls

    def chunk_slot(s, c, size):
        return pl.ds(pl.multiple_of((s * per_sup + c) * size, size), size)

    def store_tile(j, s, u, intra, lhs, ns, egls):
        rows = pl.ds(pl.multiple_of(s * sup, sup), sup)
        u_scr[j, rows, :] = u
        intra_scr[j, rows, :] = intra
        for c in range(per_sup):
            lhs_scr[j, chunk_slot(s, c, lhs_rows), :] = lhs[c]
            n_scr[j, chunk_slot(s, c, GDN_HEAD_DIM), :] = ns[c]
            egl_scr[j, chunk_slot(s, c, 8), :] = egls[c]

    lax.fori_loop(0, nsup // tiles_per_step, prepare, 0)

    gnw = gnw_ref[...]

    def scan(s, states):
        r0 = pl.multiple_of(s * sup, sup)
        states = list(states)
        v_parts = [[] for _ in range(hps)]
        for c in range(per_sup):
            rows = pl.ds(r0 + c * chunk, chunk)
            r = [jnp.dot(lhs_scr[j, chunk_slot(s, c, lhs_rows), :], states[j].astype(BF16),
                         preferred_element_type=F32) for j in range(hps)]
            outs = []
            for j in range(hps):
                v_parts[j].append(u_scr[j, rows, :] - r[j][:chunk])
                v_sup = jnp.concatenate(
                    v_parts[j] + [jnp.zeros(((per_sup - 1 - c) * chunk, LANES), F32)] * (c < per_sup - 1),
                    axis=0)
                o = r[j][chunk:2 * chunk] + jnp.dot(intra_scr[j, rows, :], v_sup.astype(BF16),
                                                    preferred_element_type=F32)
                egl = jnp.broadcast_to(egl_scr[j, chunk_slot(s, c, 8), :][0:1, :], (GDN_HEAD_DIM, LANES))
                states[j] = states[j] * egl - r[j][2 * chunk:] + n_scr[j, chunk_slot(s, c, GDN_HEAD_DIM), :]
                outs.append(o * lax.rsqrt(jnp.mean(o * o, axis=-1, keepdims=True) + RMS_EPS) * gnw)
            o_ref[0, rows, :] = jnp.concatenate(outs, axis=1) * zgate_ref[0, rows, :]
        return tuple(states)

    lax.fori_loop(0, nsup, scan, tuple(jnp.zeros((GDN_HEAD_DIM, GDN_HEAD_DIM), F32) for _ in range(hps)))


def _gdn(act_d, gates, gdn_norm_w, batch, seq):
    hps = GDN_HEADS_PER_STEP
    ngroup = GDN_HEADS // hps
    width = hps * GDN_HEAD_DIM
    assert GDN_PREP_GROUP % hps == 0 and (seq // GDN_SUPER) % (GDN_PREP_GROUP // hps) == 0
    act_d = act_d.reshape(batch, seq, 4 * GDN_WIDTH)
    gates = gates.reshape(batch, seq, LANES)

    def col_block(base):
        return pl.BlockSpec((1, seq, width), lambda b, g: (b, 0, base + g))

    def per_head(rows, lanes, dtype):
        return pltpu.VMEM((hps, rows, lanes), dtype)

    nchunk = seq // GDN_CHUNK
    scratch = [
        pltpu.VMEM((seq, LANES), F32),
        per_head(seq, LANES, F32),
        per_head(nchunk * (2 * GDN_CHUNK + GDN_HEAD_DIM), LANES, BF16),
        per_head(nchunk * GDN_HEAD_DIM, LANES, F32),
        per_head(nchunk * 8, LANES, F32),
        per_head(seq, GDN_SUPER, BF16),
    ]
    return pl.pallas_call(
        _gdn_kernel,
        grid=(batch, ngroup),
        in_specs=[col_block(0), col_block(ngroup), col_block(2 * ngroup), col_block(3 * ngroup),
                  pl.BlockSpec((1, seq, LANES), lambda b, g: (b, 0, 0)),
                  pl.BlockSpec((1, LANES), lambda b, g: (0, 0))],
        out_specs=pl.BlockSpec((1, seq, width), lambda b, g: (b, 0, g)),
        scratch_shapes=scratch,
        out_shape=jax.ShapeDtypeStruct((batch, seq, GDN_WIDTH), F32),
        compiler_params=pltpu.CompilerParams(
            dimension_semantics=("arbitrary", "arbitrary"), vmem_limit_bytes=VMEM_LIMIT_BYTES),
        name="gdn",
    )(act_d, act_d, act_d, act_d, gates, gdn_norm_w)


def _outproj_kernel(x_ref, ma_ref, md_ref, wa_ref, wd_ref, nw_ref, o_ref):
    y = (jnp.dot(ma_ref[...].astype(BF16), wa_ref[...], preferred_element_type=F32)
         + jnp.dot(md_ref[...].astype(BF16), wd_ref[...], preferred_element_type=F32))
    y = y * lax.rsqrt(jnp.mean(y * y, axis=-1, keepdims=True) + RMS_EPS) * nw_ref[...]
    o_ref[...] = x_ref[...] + y


def _outproj(x2, mix_a, mix_d, w_a, w_d, norm_w, tm):
    n = x2.shape[0]
    return pl.pallas_call(
        _outproj_kernel,
        grid=(n // tm,),
        in_specs=[
            pl.BlockSpec((tm, D_MODEL), lambda i: (i, 0)),
            pl.BlockSpec((tm, MOBA_WIDTH), lambda i: (i, 0)),
            pl.BlockSpec((tm, GDN_WIDTH), lambda i: (i, 0)),
            pl.BlockSpec(w_a.shape, lambda i: (0, 0)),
            pl.BlockSpec(w_d.shape, lambda i: (0, 0)),
            pl.BlockSpec((1, D_MODEL), lambda i: (0, 0)),
        ],
        out_specs=pl.BlockSpec((tm, D_MODEL), lambda i: (i, 0)),
        out_shape=jax.ShapeDtypeStruct((n, D_MODEL), F32),
        compiler_params=pltpu.CompilerParams(
            dimension_semantics=("arbitrary",), vmem_limit_bytes=VMEM_LIMIT_BYTES),
        name="outproj",
    )(x2, mix_a, mix_d, w_a, w_d, norm_w)


def _alibi_slopes():
    start = 2.0 ** (-8.0 / MOBA_HEADS)
    slopes = np.asarray(start ** np.arange(1, MOBA_HEADS + 1), dtype=np.float32)
    return jnp.asarray(slopes)


def _layer(x, norm_pre_w, w_in, conv_w, a_log, dt_bias, gdn_norm_w, w_out, norm_post_w):
    batch, seq, _ = x.shape
    assert seq % MOBA_BLOCK == 0 and seq % GDN_SUPER == 0
    n = batch * seq
    tm = 512
    assert n % tm == 0
    x2 = x.reshape(n, D_MODEL)
    n_main = 4 * MOBA_WIDTH + 4 * GDN_WIDTH
    w_main = w_in[:, :n_main].astype(BF16)
    w_tail = jnp.pad(w_in[:, n_main:], ((0, 0), (0, LANES - 2 * GDN_HEADS))).astype(BF16)
    to_decay_lanes = (GDN_HEADS, LANES - 2 * GDN_HEADS)
    a_log_lanes = jnp.pad(a_log.astype(F32), to_decay_lanes).reshape(1, LANES)
    dt_bias_lanes = jnp.pad(dt_bias.astype(F32), to_decay_lanes).reshape(1, LANES)
    proj_a, act_d, gates = _inproj(x2, norm_pre_w.reshape(1, D_MODEL), w_main, w_tail, conv_w.astype(F32),
                                   a_log_lanes, dt_bias_lanes, tm, seq)
    mix_a = _moba(_alibi_slopes(), proj_a, batch, seq)
    mix_d = _gdn(act_d, gates, gdn_norm_w.reshape(1, GDN_HEAD_DIM), batch, seq)
    w_o = w_out.astype(BF16)
    out = _outproj(x2, mix_a.reshape(n, MOBA_WIDTH), mix_d.reshape(n, GDN_WIDTH),
                   w_o[:MOBA_WIDTH], w_o[MOBA_WIDTH:], norm_post_w.reshape(1, D_MODEL), tm)
    return out.reshape(batch, seq, D_MODEL)


def kernel(x, norm_pre_w, w_in, conv_w, a_log, dt_bias, gdn_norm_w, w_out, norm_post_w):
    for layer in range(norm_pre_w.shape[0]):
        x = _layer(x, norm_pre_w[layer], w_in[layer], conv_w[layer], a_log[layer], dt_bias[layer],
                   gdn_norm_w[layer], w_out[layer], norm_post_w[layer])
    return x
```

```python
import functools

import numpy as np
import jax
import jax.numpy as jnp
from jax import lax
from jax.experimental import pallas as pl
from jax.experimental.pallas import tpu as pltpu

F32 = jnp.float32
BF16 = jnp.bfloat16

D_MODEL = 1024
MOBA_WIDTH = 512
MOBA_HEAD_DIM = 64
MOBA_HEADS = MOBA_WIDTH // MOBA_HEAD_DIM
MOBA_BLOCK = 256
MOBA_TOPK = 3
GDN_WIDTH = 512
GDN_HEAD_DIM = 128
GDN_HEADS = GDN_WIDTH // GDN_HEAD_DIM
GDN_CONV = 4
GDN_CHUNK = 128
GDN_SUPER = 256
GDN_HEADS_PER_STEP = 2
PACK_BLOCK = 64
GDN_PREP_GROUP = 8
RMS_EPS = 1e-6
NEG_INF = -1e30

LANES = 128
VMEM_LIMIT_BYTES = 48 * 1024 * 1024

NT_DIMS = (((1,), (1,)), ((), ()))
TN_DIMS = (((0,), (0,)), ((), ()))


def _dot(a, b):
    return jnp.dot(a.astype(BF16), b.astype(BF16), preferred_element_type=F32)


def _dot_nt(a, b):
    return lax.dot_general(a.astype(BF16), b.astype(BF16), NT_DIMS, preferred_element_type=F32)


def _silu(x):
    return x / (1.0 + jnp.exp(-x))


def _l2_normalize(x):
    return x * lax.rsqrt(jnp.sum(x * x, axis=-1, keepdims=True) + RMS_EPS)


def _inproj_kernel(tiles_per_seq, x_ref, nw_ref, wm_ref, wt_ref, cw_ref, alog_ref, dtb_ref,
                   moba_ref, gdn_ref, gate_ref, pad_scr):
    tm = x_ref.shape[0]
    x = x_ref[...]
    h = x * lax.rsqrt(jnp.mean(x * x, axis=-1, keepdims=True) + RMS_EPS) * nw_ref[...]
    hb = h.astype(BF16)
    half = moba_ref.shape[-1]

    conv_width = pad_scr.shape[1]
    first = (pl.program_id(0) % tiles_per_seq) == 0

    @pl.when(first)
    def _():
        pad_scr[0:8, :] = jnp.zeros((8, conv_width), F32)

    @pl.when(jnp.logical_not(first))
    def _():
        pad_scr[0:8, :] = pad_scr[tm:tm + 8, :]

    for part in range(conv_width // GDN_WIDTH):
        cols = slice(part * GDN_WIDTH, (part + 1) * GDN_WIDTH)
        d = jnp.dot(hb, wm_ref[:, half + part * GDN_WIDTH:half + (part + 1) * GDN_WIDTH],
                    preferred_element_type=F32)
        pad_scr[8:, cols] = d
        acc = d * cw_ref[GDN_CONV - 1:GDN_CONV, cols]
        for j in range(GDN_CONV - 1):
            shift = GDN_CONV - 1 - j
            acc = acc + pad_scr[8 - shift:8 - shift + tm, cols] * cw_ref[j:j + 1, cols]
        act = _silu(acc)
        for hh in range(GDN_HEADS):
            head_cols = slice(hh * GDN_HEAD_DIM, (hh + 1) * GDN_HEAD_DIM)
            out_cols = slice(part * GDN_WIDTH + hh * GDN_HEAD_DIM, part * GDN_WIDTH + (hh + 1) * GDN_HEAD_DIM)
            if part == 0:
                gdn_ref[:, out_cols] = _l2_normalize(act[:, head_cols]) * (GDN_HEAD_DIM ** -0.5)
            elif part == 1:
                gdn_ref[:, out_cols] = _l2_normalize(act[:, head_cols])
            else:
                gdn_ref[:, out_cols] = act[:, head_cols]
        moba_ref[:, cols] = jnp.dot(hb, wm_ref[:, cols], preferred_element_type=F32)
    z = jnp.dot(hb, wm_ref[:, half + conv_width:], preferred_element_type=F32)
    gdn_ref[:, conv_width:] = _silu(z)
    ba = jnp.dot(hb, wt_ref[...], preferred_element_type=F32)
    moba_ref[:, conv_width:] = jnp.dot(hb, wm_ref[:, conv_width:half], preferred_element_type=F32)

    lane = lax.broadcasted_iota(jnp.int32, (1, LANES), 1)
    pre = ba + dtb_ref[...]
    softplus = jnp.maximum(pre, 0.0) + jnp.log(1.0 + jnp.exp(-jnp.abs(pre)))
    gate_ref[...] = jnp.where(lane < GDN_HEADS, 1.0 / (1.0 + jnp.exp(-ba)),
                              -jnp.exp(alog_ref[...]) * softplus)


def _inproj(x2, norm_w, w_main, w_tail, conv_w, a_log_lanes, dt_bias_lanes, tm, seq):
    n = x2.shape[0]
    half = w_main.shape[1] // 2
    assert seq % tm == 0 and tm % 8 == 0
    return pl.pallas_call(
        functools.partial(_inproj_kernel, seq // tm),
        grid=(n // tm,),
        in_specs=[
            pl.BlockSpec((tm, D_MODEL), lambda i: (i, 0)),
            pl.BlockSpec((1, D_MODEL), lambda i: (0, 0)),
            pl.BlockSpec(w_main.shape, lambda i: (0, 0)),
            pl.BlockSpec(w_tail.shape, lambda i: (0, 0)),
            pl.BlockSpec(conv_w.shape, lambda i: (0, 0)),
            pl.BlockSpec((1, LANES), lambda i: (0, 0)),
            pl.BlockSpec((1, LANES), lambda i: (0, 0)),
        ],
        out_specs=[
            pl.BlockSpec((tm, half), lambda i: (i, 0)),
            pl.BlockSpec((tm, half), lambda i: (i, 0)),
            pl.BlockSpec((tm, LANES), lambda i: (i, 0)),
        ],
        out_shape=[
            jax.ShapeDtypeStruct((n, half), F32),
            jax.ShapeDtypeStruct((n, half), F32),
            jax.ShapeDtypeStruct((n, LANES), F32),
        ],
        scratch_shapes=[pltpu.VMEM((tm + 8, conv_w.shape[1]), F32)],
        compiler_params=pltpu.CompilerParams(
            dimension_semantics=("arbitrary",), vmem_limit_bytes=VMEM_LIMIT_BYTES),
        name="inproj",
    )(x2, norm_w, w_main, w_tail, conv_w, a_log_lanes, dt_bias_lanes)


MOBA_MAX_BLOCKS = 8
FEAT_POS_HI = MOBA_MAX_BLOCKS
FEAT_POS_LO = MOBA_MAX_BLOCKS + 1
FEAT_ONE_HI = MOBA_MAX_BLOCKS + 2
FEAT_ONE_LO = MOBA_MAX_BLOCKS + 3


def _moba_features(seq):
    blk = MOBA_BLOCK
    pos = np.arange(seq)
    kfeat = np.zeros((2, seq, LANES), np.float32)
    qfeat = np.zeros((2, seq, LANES), np.float32)
    for h in range(2):
        base = MOBA_HEAD_DIM * (1 - h)
        kfeat[h, pos, base + pos // blk] = 1.0
        kfeat[h, :, base + FEAT_POS_HI] = blk * (pos // blk)
        kfeat[h, :, base + FEAT_POS_LO] = pos % blk
        kfeat[h, :, base + FEAT_ONE_HI] = 1.0
        kfeat[h, :, base + FEAT_ONE_LO] = 1.0
        qfeat[h, :, base + FEAT_POS_HI] = 1.0
        qfeat[h, :, base + FEAT_POS_LO] = 1.0
        qfeat[h, :, base + FEAT_ONE_HI] = -blk * (pos // blk)
        qfeat[h, :, base + FEAT_ONE_LO] = -(pos % blk)
    return jnp.asarray(kfeat), jnp.asarray(qfeat)


def _moba_kernel(slopes_ref, q_ref, k_ref, v_ref, z_ref, kfeat_ref, qfeat_ref, o_ref,
                 kaug_scr, qaug_scr, vaug_scr, logit_scr):
    seq = q_ref.shape[1]
    blk = MOBA_BLOCK
    nb = seq // blk
    pair = pl.program_id(1)
    lane = lax.broadcasted_iota(jnp.int32, (1, LANES), 1)
    head_masks = [(lane >= MOBA_HEAD_DIM * h) & (lane < MOBA_HEAD_DIM * (h + 1)) for h in range(2)]
    feat_lane = [lane - MOBA_HEAD_DIM * (1 - h) for h in range(2)]
    slopes = [slopes_ref[pair * 2 + h] for h in range(2)]

    k_all = k_ref[0]
    vaug_scr[:, :LANES] = v_ref[0].astype(BF16)
    vaug_scr[:, LANES:] = jnp.ones((seq, LANES), BF16)
    row8 = lax.broadcasted_iota(jnp.int32, (MOBA_MAX_BLOCKS, LANES), 0)
    kbar = jnp.zeros((MOBA_MAX_BLOCKS, LANES), F32)
    for n in range(nb):
        kbar = jnp.where(row8 == n, jnp.mean(k_all[n * blk:(n + 1) * blk], axis=0, keepdims=True), kbar)
    to_feat_lanes = [jnp.where(feat_lane[h] == row8, 1.0, 0.0) for h in range(2)]
    for h in range(2):
        kaug_scr[h] = jnp.where(head_masks[h], k_all, kfeat_ref[h]).astype(BF16)
    row = lax.broadcasted_iota(jnp.int32, (blk, blk), 0)
    col = lax.broadcasted_iota(jnp.int32, (blk, blk), 1)
    causal = row >= col

    q_all = q_ref[0]
    ranked = (MOBA_TOPK + 1) * blk
    block_id = lax.broadcasted_iota(jnp.int32, (MOBA_MAX_BLOCKS, seq - ranked), 0)
    own = jnp.right_shift(lax.broadcasted_iota(jnp.int32, (MOBA_MAX_BLOCKS, seq - ranked), 1) + ranked,
                          blk.bit_length() - 1)
    gates = [lax.dot_general(kbar, jnp.where(head_masks[h], q_all[ranked:], 0.0), NT_DIMS,
                             precision=lax.Precision.HIGHEST, preferred_element_type=F32)
             for h in range(2)]
    cnt = [jnp.zeros((MOBA_MAX_BLOCKS, seq - ranked), F32) for _ in range(2)]
    for other in range(nb - 1):
        for h in range(2):
            g_other = gates[h][other:other + 1, :]
            before = (g_other > gates[h]) | ((g_other == gates[h]) & (block_id > other))
            cnt[h] = cnt[h] + jnp.where(before & (own > other), 1.0, 0.0)
    for h in range(2):
        dropped = jnp.where((cnt[h] >= MOBA_TOPK) & (block_id < own), NEG_INF, 0.0)
        feat = slopes[h] * qfeat_ref[h]
        scaled = q_all * (MOBA_HEAD_DIM ** -0.5)
        qaug_scr[h, :ranked, :] = jnp.where(head_masks[h], scaled[:ranked], feat[:ranked]).astype(BF16)
        sel_feat = lax.dot_general(dropped, to_feat_lanes[h], TN_DIMS, preferred_element_type=F32)
        qaug_scr[h, ranked:, :] = jnp.where(head_masks[h], scaled[ranked:],
                                            feat[ranked:] + sel_feat).astype(BF16)

    for i in range(nb):
        rows = slice(i * blk, (i + 1) * blk)
        q_aug = [qaug_scr[h, rows, :] for h in range(2)]

        mx = [None, None]
        for n in range(i + 1):
            for h in range(2):
                tile = lax.dot_general(q_aug[h], kaug_scr[h, n * blk:(n + 1) * blk, :], NT_DIMS,
                                       preferred_element_type=F32)
                if n == i:
                    tile = jnp.where(causal, tile, NEG_INF)
                logit_scr[h, n] = tile
                halves = jnp.maximum(tile[:, :LANES], tile[:, LANES:])
                mx[h] = halves if mx[h] is None else jnp.maximum(mx[h], halves)
        m = [jnp.max(mx[h], axis=1, keepdims=True) for h in range(2)]
        acc = jnp.zeros((2 * blk, 2 * LANES), F32)
        for n in range(i + 1):
            p = [jnp.exp(logit_scr[h, n] - m[h]) for h in range(2)]
            acc = acc + jnp.dot(jnp.concatenate(p, axis=0).astype(BF16), vaug_scr[n * blk:(n + 1) * blk, :],
                                preferred_element_type=F32)
        normed = acc[:, :LANES] / acc[:, LANES:]
        out = jnp.where(head_masks[0], normed[:blk], normed[blk:])
        o_ref[0, rows, :] = (out * _silu(z_ref[0, rows, :])).astype(o_ref.dtype)


def _moba(slopes, proj_a, batch, seq):
    npair = MOBA_WIDTH // LANES
    proj_a = proj_a.reshape(batch, seq, 4 * MOBA_WIDTH)
    nb = seq // MOBA_BLOCK
    assert nb <= MOBA_MAX_BLOCKS, "block-indicator feature lanes"
    kfeat, qfeat = _moba_features(seq)

    def col_block(base):
        return pl.BlockSpec((1, seq, LANES), lambda b, p, *_: (b, 0, base + p))

    return pl.pallas_call(
        _moba_kernel,
        grid_spec=pltpu.PrefetchScalarGridSpec(
            num_scalar_prefetch=1,
            grid=(batch, npair),
            in_specs=[col_block(0), col_block(npair), col_block(2 * npair), col_block(3 * npair),
                      pl.BlockSpec(kfeat.shape, lambda b, p, *_: (0, 0, 0)),
                      pl.BlockSpec(qfeat.shape, lambda b, p, *_: (0, 0, 0))],
            out_specs=pl.BlockSpec((1, seq, LANES), lambda b, p, *_: (b, 0, p)),
            scratch_shapes=[pltpu.VMEM((2, seq, LANES), BF16),
                            pltpu.VMEM((2, seq, LANES), BF16),
                            pltpu.VMEM((seq, 2 * LANES), BF16),
                            pltpu.VMEM((2, nb, MOBA_BLOCK, MOBA_BLOCK), F32)],
        ),
        out_shape=jax.ShapeDtypeStruct((batch, seq, MOBA_WIDTH), BF16),
        compiler_params=pltpu.CompilerParams(
            dimension_semantics=("arbitrary", "arbitrary"), vmem_limit_bytes=VMEM_LIMIT_BYTES),
        name="moba",
    )(slopes, proj_a, proj_a, proj_a, proj_a, kfeat, qfeat)


def _each(fn, *lists):
    return [fn(*args) for args in zip(*lists)]


def _unit_lower_inverse(a_list, row, col, chunk):
    size = a_list[0].shape[0]
    assert chunk in (PACK_BLOCK, 2 * PACK_BLOCK) and size % (2 * PACK_BLOCK) == 0

    def pack(m, b):
        out = m[0:b]
        for r in range(b, size, b):
            out = out + m[r:r + b]
        return out

    def unpack(pk, b):
        tiled = pltpu.repeat(pk, size // b, axis=0)
        return jnp.where(jnp.right_shift(row, b.bit_length() - 1) == jnp.right_shift(col, b.bit_length() - 1),
                         tiled, 0.0)

    b = PACK_BLOCK
    prow = lax.broadcasted_iota(jnp.int32, (b, size), 0)
    pcol = jnp.bitwise_and(lax.broadcasted_iota(jnp.int32, (b, size), 1), b - 1)

    def same_block(shift):
        return jnp.right_shift(prow, shift) == jnp.right_shift(pcol, shift)

    in_pack = jnp.right_shift(row, b.bit_length() - 1) == jnp.right_shift(col, b.bit_length() - 1)
    a_p = _each(lambda a: pack(jnp.where(in_pack, a, 0.0), b), a_list)
    eye_p = jnp.where(prow == pcol, 1.0, 0.0)
    inner = same_block(4)
    d_p = _each(lambda ap: jnp.where(inner, ap, 0.0), a_p)
    x_p = _each(lambda dp: eye_p - dp, d_p)
    p_bd = _each(lambda dp: unpack(dp, b), d_p)
    p_p = _each(_dot, d_p, p_bd)
    for step in range(3):
        p_bd = _each(lambda pp: unpack(pp, b), p_p)
        x_p = _each(lambda xp, pb: xp + _dot(xp, pb), x_p, p_bd)
        if step < 2:
            p_p = _each(_dot, p_p, p_bd)
    shift = 5
    while (1 << shift) <= b:
        outer = same_block(shift)
        coupling_mask = outer & jnp.logical_not(inner)
        x_bd = _each(lambda xp: unpack(xp, b), x_p)
        lx_p = _each(lambda ap, xb: _dot(jnp.where(coupling_mask, ap, 0.0), xb), a_p, x_bd)
        x_p = _each(lambda xp, lp: xp - _dot(xp, unpack(lp, b)), x_p, lx_p)
        inner = outer
        shift += 1
    x_bd = _each(lambda xp: unpack(xp, b), x_p)
    if chunk == b:
        return x_bd
    b2 = 2 * b
    in_pack2 = jnp.right_shift(row, b2.bit_length() - 1) == jnp.right_shift(col, b2.bit_length() - 1)
    x_p2 = _each(lambda xb: pack(xb, b2), x_bd)
    l_p2 = _each(lambda a: pack(jnp.where(in_pack2 & jnp.logical_not(in_pack), a, 0.0), b2), a_list)
    lx_p2 = _each(_dot, l_p2, x_bd)
    x_p2 = _each(lambda xp, lp: xp - _dot(xp, unpack(lp, b2)), x_p2, lx_p2)
    return _each(lambda xp: unpack(xp, b2), x_p2)


def _gdn_kernel(q_ref, k_ref, v_ref, zgate_ref, ba_ref, gnw_ref, o_ref,
                gate_scr, u_scr, lhs_scr, n_scr, egl_scr, intra_scr):
    seq = q_ref.shape[1]
    sup = GDN_SUPER
    chunk = GDN_CHUNK
    per_sup = sup // chunk
    nsup = seq // sup
    hps = GDN_HEADS_PER_STEP
    lhs_rows = 2 * chunk + GDN_HEAD_DIM
    group = pl.program_id(1)
    lane = lax.broadcasted_iota(jnp.int32, (1, LANES), 1)
    row = lax.broadcasted_iota(jnp.int32, (sup, sup), 0)
    col = lax.broadcasted_iota(jnp.int32, (sup, sup), 1)
    chunk_shift = chunk.bit_length() - 1
    same_chunk = jnp.right_shift(row, chunk_shift) == jnp.right_shift(col, chunk_shift)
    causal = same_chunk & (row >= col)
    strict = same_chunk & (row > col)

    @pl.when(group == 0)
    def _():
        tri = jnp.where(causal, 1.0, 0.0)
        for s in range(nsup):
            rows = slice(s * sup, (s + 1) * sup)
            gates = ba_ref[0, rows, :]
            gc = jnp.dot(tri, gates, precision=lax.Precision.HIGHEST, preferred_element_type=F32)
            gate_scr[rows, :] = jnp.where(lane < GDN_HEADS, gates, gc)

    tiles_per_step = GDN_PREP_GROUP // hps

    def prepare(step, carry):
        loaded = []
        for t in range(tiles_per_step):
            rows = pl.ds(pl.multiple_of((step * tiles_per_step + t) * sup, sup), sup)
            for j in range(hps):
                cols = slice(j * LANES, (j + 1) * LANES)
                loaded.append((group * hps + j, gate_scr[rows, :], q_ref[0, rows, cols], k_ref[0, rows, cols],
                               v_ref[0, rows, cols]))
        results = prepare_tiles(*zip(*loaded))
        for t in range(tiles_per_step):
            for j in range(hps):
                store_tile(j, step * tiles_per_step + t, *[r[t * hps + j] for r in results])
        return carry

    def decay_terms(head, gates):
        beta = jnp.sum(jnp.where(lane == head, gates, 0.0), axis=1, keepdims=True)
        gc = jnp.sum(jnp.where(lane == GDN_HEADS + head, gates, 0.0), axis=1, keepdims=True)
        gc_b = jnp.broadcast_to(gc, (sup, LANES))
        gc_row = jnp.concatenate([gc_b[0:LANES].T, gc_b[LANES:].T], axis=1)[0:1, :]
        decay = jnp.exp(jnp.where(causal, gc - gc_row, NEG_INF))
        gl_b = jnp.concatenate(
            [jnp.broadcast_to(gc_b[(c + 1) * chunk - 1:(c + 1) * chunk, :], (chunk, LANES))
             for c in range(per_sup)], axis=0)
        return beta, decay, jnp.exp(gc), jnp.exp(gl_b - gc_b), jnp.exp(gl_b)

    def state_terms(k_dec, w, u, q_dec, egl):
        lhs, ns, egls = [], [], []
        for c in range(per_sup):
            cr = slice(c * chunk, (c + 1) * chunk)
            pn = lax.dot_general(k_dec[cr].astype(BF16),
                                 jnp.concatenate([w[cr], u[cr]], axis=1).astype(BF16),
                                 TN_DIMS, preferred_element_type=F32)
            lhs.append(jnp.concatenate([w[cr], q_dec[cr], pn[:, :GDN_HEAD_DIM]], axis=0).astype(BF16))
            ns.append(pn[:, GDN_HEAD_DIM:])
            egls.append(egl[c * chunk:c * chunk + 8])
        return lhs, ns, egls

    def prepare_tiles(heads, gates, qb, kb, vb):
        beta, decay, egc, e_tail, egl = zip(*_each(decay_terms, heads, gates))
        k_beta = _each(lambda k, b: k * b, kb, beta)
        a = _each(lambda kbt, k, dc: jnp.where(strict, _dot_nt(kbt, k) * dc, 0.0), k_beta, kb, decay)
        t_inv = _unit_lower_inverse(a, row, col, chunk)
        wu = _each(lambda t, kbt, e, v, b: _dot(t, jnp.concatenate([kbt * e, v * b], axis=1)),
                   t_inv, k_beta, egc, vb, beta)
        w = [x[:, :GDN_HEAD_DIM] for x in wu]
        u = [x[:, GDN_HEAD_DIM:] for x in wu]
        intra = _each(lambda q, k, dc: jnp.where(causal, _dot_nt(q, k) * dc, 0.0).astype(BF16),
                      qb, kb, decay)
        q_dec = _each(lambda q, e: q * e, qb, egc)
        k_dec = _each(lambda k, e: k * e, kb, e_tail)
        lhs, ns, egls = zip(*_each(state_terms, k_dec, w, u, q_dec, egl))
        return u, intra, lhs, ns, egls

    def chunk_slot(s, c, size):
        return pl.ds(pl.multiple_of((s * per_sup + c) * size, size), size)

    def store_tile(j, s, u, intra, lhs, ns, egls):
        rows = pl.ds(pl.multiple_of(s * sup, sup), sup)
        u_scr[j, rows, :] = u
        intra_scr[j, rows, :] = intra
        for c in range(per_sup):
            lhs_scr[j, chunk_slot(s, c, lhs_rows), :] = lhs[c]
            n_scr[j, chunk_slot(s, c, GDN_HEAD_DIM), :] = ns[c]
            egl_scr[j, chunk_slot(s, c, 8), :] = egls[c]

    lax.fori_loop(0, nsup // tiles_per_step, prepare, 0)

    gnw = gnw_ref[...]

    def scan(s, states):
        r0 = pl.multiple_of(s * sup, sup)
        states = list(states)
        v_parts = [[] for _ in range(hps)]
        for c in range(per_sup):
            rows = pl.ds(r0 + c * chunk, chunk)
            r = [jnp.dot(lhs_scr[j, chunk_slot(s, c, lhs_rows), :], states[j].astype(BF16),
                         preferred_element_type=F32) for j in range(hps)]
            outs = []
            for j in range(hps):
                v_parts[j].append(u_scr[j, rows, :] - r[j][:chunk])
                v_sup = jnp.concatenate(
                    v_parts[j] + [jnp.zeros(((per_sup - 1 - c) * chunk, LANES), F32)] * (c < per_sup - 1),
                    axis=0)
                o = r[j][chunk:2 * chunk] + jnp.dot(intra_scr[j, rows, :], v_sup.astype(BF16),
                                                    preferred_element_type=F32)
                egl = jnp.broadcast_to(egl_scr[j, chunk_slot(s, c, 8), :][0:1, :], (GDN_HEAD_DIM, LANES))
                states[j] = states[j] * egl - r[j][2 * chunk:] + n_scr[j, chunk_slot(s, c, GDN_HEAD_DIM), :]
                outs.append(o * lax.rsqrt(jnp.mean(o * o, axis=-1, keepdims=True) + RMS_EPS) * gnw)
            o_ref[0, rows, :] = (jnp.concatenate(outs, axis=1) * zgate_ref[0, rows, :]).astype(o_ref.dtype)
        return tuple(states)

    lax.fori_loop(0, nsup, scan, tuple(jnp.zeros((GDN_HEAD_DIM, GDN_HEAD_DIM), F32) for _ in range(hps)))


def _gdn(act_d, gates, gdn_norm_w, batch, seq):
    hps = GDN_HEADS_PER_STEP
    ngroup = GDN_HEADS // hps
    width = hps * GDN_HEAD_DIM
    assert GDN_PREP_GROUP % hps == 0 and (seq // GDN_SUPER) % (GDN_PREP_GROUP // hps) == 0
    act_d = act_d.reshape(batch, seq, 4 * GDN_WIDTH)
    gates = gates.reshape(batch, seq, LANES)

    def col_block(base):
        return pl.BlockSpec((1, seq, width), lambda b, g: (b, 0, base + g))

    def per_head(rows, lanes, dtype):
        return pltpu.VMEM((hps, rows, lanes), dtype)

    nchunk = seq // GDN_CHUNK
    scratch = [
        pltpu.VMEM((seq, LANES), F32),
        per_head(seq, LANES, F32),
        per_head(nchunk * (2 * GDN_CHUNK + GDN_HEAD_DIM), LANES, BF16),
        per_head(nchunk * GDN_HEAD_DIM, LANES, F32),
        per_head(nchunk * 8, LANES, F32),
        per_head(seq, GDN_SUPER, BF16),
    ]
    return pl.pallas_call(
        _gdn_kernel,
        grid=(batch, ngroup),
        in_specs=[col_block(0), col_block(ngroup), col_block(2 * ngroup), col_block(3 * ngroup),
                  pl.BlockSpec((1, seq, LANES), lambda b, g: (b, 0, 0)),
                  pl.BlockSpec((1, LANES), lambda b, g: (0, 0))],
        out_specs=pl.BlockSpec((1, seq, width), lambda b, g: (b, 0, g)),
        scratch_shapes=scratch,
        out_shape=jax.ShapeDtypeStruct((batch, seq, GDN_WIDTH), BF16),
        compiler_params=pltpu.CompilerParams(
            dimension_semantics=("arbitrary", "arbitrary"), vmem_limit_bytes=VMEM_LIMIT_BYTES),
        name="gdn",
    )(act_d, act_d, act_d, act_d, gates, gdn_norm_w)


def _outproj_kernel(x_ref, ma_ref, md_ref, wa_ref, wd_ref, nw_ref, o_ref):
    y = (jnp.dot(ma_ref[...], wa_ref[...], preferred_element_type=F32)
         + jnp.dot(md_ref[...], wd_ref[...], preferred_element_type=F32))
    y = y * lax.rsqrt(jnp.mean(y * y, axis=-1, keepdims=True) + RMS_EPS) * nw_ref[...]
    o_ref[...] = x_ref[...] + y


def _outproj(x2, mix_a, mix_d, w_a, w_d, norm_w, tm):
    n = x2.shape[0]
    return pl.pallas_call(
        _outproj_kernel,
        grid=(n // tm,),
        in_specs=[
            pl.BlockSpec((tm, D_MODEL), lambda i: (i, 0)),
            pl.BlockSpec((tm, MOBA_WIDTH), lambda i: (i, 0)),
            pl.BlockSpec((tm, GDN_WIDTH), lambda i: (i, 0)),
            pl.BlockSpec(w_a.shape, lambda i: (0, 0)),
            pl.BlockSpec(w_d.shape, lambda i: (0, 0)),
            pl.BlockSpec((1, D_MODEL), lambda i: (0, 0)),
        ],
        out_specs=pl.BlockSpec((tm, D_MODEL), lambda i: (i, 0)),
        out_shape=jax.ShapeDtypeStruct((n, D_MODEL), F32),
        compiler_params=pltpu.CompilerParams(
            dimension_semantics=("arbitrary",), vmem_limit_bytes=VMEM_LIMIT_BYTES),
        name="outproj",
    )(x2, mix_a, mix_d, w_a, w_d, norm_w)


def _alibi_slopes():
    start = 2.0 ** (-8.0 / MOBA_HEADS)
    slopes = np.asarray(start ** np.arange(1, MOBA_HEADS + 1), dtype=np.float32)
    return jnp.asarray(slopes)


def _layer(x, norm_pre_w, w_in, conv_w, a_log, dt_bias, gdn_norm_w, w_out, norm_post_w):
    batch, seq, _ = x.shape
    assert seq % MOBA_BLOCK == 0 and seq % GDN_SUPER == 0
    n = batch * seq
    tm = 512
    assert n % tm == 0
    x2 = x.reshape(n, D_MODEL)
    n_main = 4 * MOBA_WIDTH + 4 * GDN_WIDTH
    w_main = w_in[:, :n_main].astype(BF16)
    w_tail = jnp.pad(w_in[:, n_main:], ((0, 0), (0, LANES - 2 * GDN_HEADS))).astype(BF16)
    to_decay_lanes = (GDN_HEADS, LANES - 2 * GDN_HEADS)
    a_log_lanes = jnp.pad(a_log.astype(F32), to_decay_lanes).reshape(1, LANES)
    dt_bias_lanes = jnp.pad(dt_bias.astype(F32), to_decay_lanes).reshape(1, LANES)
    proj_a, act_d, gates = _inproj(x2, norm_pre_w.reshape(1, D_MODEL), w_main, w_tail, conv_w.astype(F32),
                                   a_log_lanes, dt_bias_lanes, tm, seq)
    mix_a = _moba(_alibi_slopes(), proj_a, batch, seq)
    mix_d = _gdn(act_d, gates, gdn_norm_w.reshape(1, GDN_HEAD_DIM), batch, seq)
    w_o = w_out.astype(BF16)
    out = _outproj(x2, mix_a.reshape(n, MOBA_WIDTH), mix_d.reshape(n, GDN_WIDTH),
                   w_o[:MOBA_WIDTH], w_o[MOBA_WIDTH:], norm_post_w.reshape(1, D_MODEL), tm)
    return out.reshape(batch, seq, D_MODEL)


def kernel(x, norm_pre_w, w_in, conv_w, a_log, dt_bias, gdn_norm_w, w_out, norm_post_w):
    for layer in range(norm_pre_w.shape[0]):
        x = _layer(x, norm_pre_w[layer], w_in[layer], conv_w[layer], a_log[layer], dt_bias[layer],
                   gdn_norm_w[layer], w_out[layer], norm_post_w[layer])
    return x
```

```python
import functools

import numpy as np
import jax
import jax.numpy as jnp
from jax import lax
from jax.experimental import pallas as pl
from jax.experimental.pallas import tpu as pltpu

F32 = jnp.float32
BF16 = jnp.bfloat16

D_MODEL = 1024
MOBA_WIDTH = 512
MOBA_HEAD_DIM = 64
MOBA_HEADS = MOBA_WIDTH // MOBA_HEAD_DIM
MOBA_BLOCK = 256
MOBA_TOPK = 3
GDN_WIDTH = 512
GDN_HEAD_DIM = 128
GDN_HEADS = GDN_WIDTH // GDN_HEAD_DIM
GDN_CONV = 4
GDN_CHUNK = 128
GDN_SUPER = 256
GDN_HEADS_PER_STEP = 2
PACK_BLOCK = 64
GDN_PREP_GROUP = 8
RMS_EPS = 1e-6
NEG_INF = -1e30

LANES = 128
VMEM_LIMIT_BYTES = 48 * 1024 * 1024

NT_DIMS = (((1,), (1,)), ((), ()))
TN_DIMS = (((0,), (0,)), ((), ()))


def _dot(a, b):
    return jnp.dot(a.astype(BF16), b.astype(BF16), preferred_element_type=F32)


def _dot_nt(a, b):
    return lax.dot_general(a.astype(BF16), b.astype(BF16), NT_DIMS, preferred_element_type=F32)


def _silu(x):
    return x / (1.0 + jnp.exp(-x))


def _l2_normalize(x):
    return x * lax.rsqrt(jnp.sum(x * x, axis=-1, keepdims=True) + RMS_EPS)


def _inproj_kernel(tiles_per_seq, x_ref, nw_ref, wm_ref, wt_ref, cw_ref, alog_ref, dtb_ref,
                   moba_ref, gdn_ref, gate_ref, pad_scr):
    tm = x_ref.shape[0]
    x = x_ref[...]
    h = x * lax.rsqrt(jnp.mean(x * x, axis=-1, keepdims=True) + RMS_EPS) * nw_ref[...]
    hb = h.astype(BF16)
    half = moba_ref.shape[-1]

    conv_width = pad_scr.shape[1]
    first = (pl.program_id(0) % tiles_per_seq) == 0

    @pl.when(first)
    def _():
        pad_scr[0:8, :] = jnp.zeros((8, conv_width), F32)

    @pl.when(jnp.logical_not(first))
    def _():
        pad_scr[0:8, :] = pad_scr[tm:tm + 8, :]

    for part in range(conv_width // GDN_WIDTH):
        cols = slice(part * GDN_WIDTH, (part + 1) * GDN_WIDTH)
        d = jnp.dot(hb, wm_ref[:, half + part * GDN_WIDTH:half + (part + 1) * GDN_WIDTH],
                    preferred_element_type=F32)
        pad_scr[8:, cols] = d
        acc = d * cw_ref[GDN_CONV - 1:GDN_CONV, cols]
        for j in range(GDN_CONV - 1):
            shift = GDN_CONV - 1 - j
            acc = acc + pad_scr[8 - shift:8 - shift + tm, cols] * cw_ref[j:j + 1, cols]
        act = _silu(acc)
        for hh in range(GDN_HEADS):
            head_cols = slice(hh * GDN_HEAD_DIM, (hh + 1) * GDN_HEAD_DIM)
            out_cols = slice(part * GDN_WIDTH + hh * GDN_HEAD_DIM, part * GDN_WIDTH + (hh + 1) * GDN_HEAD_DIM)
            if part == 0:
                gdn_ref[:, out_cols] = _l2_normalize(act[:, head_cols]) * (GDN_HEAD_DIM ** -0.5)
            elif part == 1:
                gdn_ref[:, out_cols] = _l2_normalize(act[:, head_cols])
            else:
                gdn_ref[:, out_cols] = act[:, head_cols]
        moba_ref[:, cols] = jnp.dot(hb, wm_ref[:, cols], preferred_element_type=F32)
    z = jnp.dot(hb, wm_ref[:, half + conv_width:], preferred_element_type=F32)
    gdn_ref[:, conv_width:] = _silu(z)
    ba = jnp.dot(hb, wt_ref[...], preferred_element_type=F32)
    moba_ref[:, conv_width:] = jnp.dot(hb, wm_ref[:, conv_width:half], preferred_element_type=F32)

    lane = lax.broadcasted_iota(jnp.int32, (1, LANES), 1)
    pre = ba + dtb_ref[...]
    softplus = jnp.maximum(pre, 0.0) + jnp.log(1.0 + jnp.exp(-jnp.abs(pre)))
    gate_ref[...] = jnp.where(lane < GDN_HEADS, 1.0 / (1.0 + jnp.exp(-ba)),
                              -jnp.exp(alog_ref[...]) * softplus)


def _inproj(x2, norm_w, w_main, w_tail, conv_w, a_log_lanes, dt_bias_lanes, tm, seq):
    n = x2.shape[0]
    half = w_main.shape[1] // 2
    assert seq % tm == 0 and tm % 8 == 0
    return pl.pallas_call(
        functools.partial(_inproj_kernel, seq // tm),
        grid=(n // tm,),
        in_specs=[
            pl.BlockSpec((tm, D_MODEL), lambda i: (i, 0)),
            pl.BlockSpec((1, D_MODEL), lambda i: (0, 0)),
            pl.BlockSpec(w_main.shape, lambda i: (0, 0)),
            pl.BlockSpec(w_tail.shape, lambda i: (0, 0)),
            pl.BlockSpec(conv_w.shape, lambda i: (0, 0)),
            pl.BlockSpec((1, LANES), lambda i: (0, 0)),
            pl.BlockSpec((1, LANES), lambda i: (0, 0)),
        ],
        out_specs=[
            pl.BlockSpec((tm, half), lambda i: (i, 0)),
            pl.BlockSpec((tm, half), lambda i: (i, 0)),
            pl.BlockSpec((tm, LANES), lambda i: (i, 0)),
        ],
        out_shape=[
            jax.ShapeDtypeStruct((n, half), F32),
            jax.ShapeDtypeStruct((n, half), F32),
            jax.ShapeDtypeStruct((n, LANES), F32),
        ],
        scratch_shapes=[pltpu.VMEM((tm + 8, conv_w.shape[1]), F32)],
        compiler_params=pltpu.CompilerParams(
            dimension_semantics=("arbitrary",), vmem_limit_bytes=VMEM_LIMIT_BYTES),
        name="inproj",
    )(x2, norm_w, w_main, w_tail, conv_w, a_log_lanes, dt_bias_lanes)


MOBA_MAX_BLOCKS = 8
FEAT_POS_HI = MOBA_MAX_BLOCKS
FEAT_POS_LO = MOBA_MAX_BLOCKS + 1
FEAT_ONE_HI = MOBA_MAX_BLOCKS + 2
FEAT_ONE_LO = MOBA_MAX_BLOCKS + 3


def _moba_features(seq):
    blk = MOBA_BLOCK
    pos = np.arange(seq)
    kfeat = np.zeros((2, seq, LANES), np.float32)
    qfeat = np.zeros((2, seq, LANES), np.float32)
    for h in range(2):
        base = MOBA_HEAD_DIM * (1 - h)
        kfeat[h, pos, base + pos // blk] = 1.0
        kfeat[h, :, base + FEAT_POS_HI] = blk * (pos // blk)
        kfeat[h, :, base + FEAT_POS_LO] = pos % blk
        kfeat[h, :, base + FEAT_ONE_HI] = 1.0
        kfeat[h, :, base + FEAT_ONE_LO] = 1.0
        qfeat[h, :, base + FEAT_POS_HI] = 1.0
        qfeat[h, :, base + FEAT_POS_LO] = 1.0
        qfeat[h, :, base + FEAT_ONE_HI] = -blk * (pos // blk)
        qfeat[h, :, base + FEAT_ONE_LO] = -(pos % blk)
    return jnp.asarray(kfeat), jnp.asarray(qfeat)


def _moba_kernel(slopes_ref, q_ref, k_ref, v_ref, z_ref, kfeat_ref, qfeat_ref, o_ref,
                 kaug_scr, qaug_scr, vaug_scr, logit_scr):
    seq = q_ref.shape[1]
    blk = MOBA_BLOCK
    nb = seq // blk
    pair = pl.program_id(1)
    lane = lax.broadcasted_iota(jnp.int32, (1, LANES), 1)
    head_masks = [(lane >= MOBA_HEAD_DIM * h) & (lane < MOBA_HEAD_DIM * (h + 1)) for h in range(2)]
    feat_lane = [lane - MOBA_HEAD_DIM * (1 - h) for h in range(2)]
    slopes = [slopes_ref[pair * 2 + h] for h in range(2)]

    k_all = k_ref[0]
    vaug_scr[:, :LANES] = v_ref[0].astype(BF16)
    vaug_scr[:, LANES:] = jnp.ones((seq, LANES), BF16)
    row8 = lax.broadcasted_iota(jnp.int32, (MOBA_MAX_BLOCKS, LANES), 0)
    kbar = jnp.zeros((MOBA_MAX_BLOCKS, LANES), F32)
    for n in range(nb):
        kbar = jnp.where(row8 == n, jnp.mean(k_all[n * blk:(n + 1) * blk], axis=0, keepdims=True), kbar)
    to_feat_lanes = [jnp.where(feat_lane[h] == row8, 1.0, 0.0) for h in range(2)]
    for h in range(2):
        kaug_scr[:, h * LANES:(h + 1) * LANES] = jnp.where(head_masks[h], k_all, kfeat_ref[h]).astype(BF16)
        qaug_scr[h, :, (1 - h) * LANES:(2 - h) * LANES] = jnp.zeros((seq, LANES), BF16)
    row = lax.broadcasted_iota(jnp.int32, (2 * blk, blk), 0)
    col = lax.broadcasted_iota(jnp.int32, (2 * blk, blk), 1)
    causal = jnp.bitwise_and(row, blk - 1) >= col

    q_all = q_ref[0]
    ranked = (MOBA_TOPK + 1) * blk
    block_id = lax.broadcasted_iota(jnp.int32, (MOBA_MAX_BLOCKS, seq - ranked), 0)
    own = jnp.right_shift(lax.broadcasted_iota(jnp.int32, (MOBA_MAX_BLOCKS, seq - ranked), 1) + ranked,
                          blk.bit_length() - 1)
    gates = [lax.dot_general(kbar, jnp.where(head_masks[h], q_all[ranked:], 0.0), NT_DIMS,
                             precision=lax.Precision.HIGHEST, preferred_element_type=F32)
             for h in range(2)]
    cnt = [jnp.zeros((MOBA_MAX_BLOCKS, seq - ranked), F32) for _ in range(2)]
    for other in range(nb - 1):
        for h in range(2):
            g_other = gates[h][other:other + 1, :]
            before = (g_other > gates[h]) | ((g_other == gates[h]) & (block_id > other))
            cnt[h] = cnt[h] + jnp.where(before & (own > other), 1.0, 0.0)
    for h in range(2):
        dropped = jnp.where((cnt[h] >= MOBA_TOPK) & (block_id < own), NEG_INF, 0.0)
        feat = slopes[h] * qfeat_ref[h]
        scaled = q_all * (MOBA_HEAD_DIM ** -0.5)
        own_lanes = slice(h * LANES, (h + 1) * LANES)
        qaug_scr[h, :ranked, own_lanes] = jnp.where(head_masks[h], scaled[:ranked], feat[:ranked]).astype(BF16)
        sel_feat = lax.dot_general(dropped, to_feat_lanes[h], TN_DIMS, preferred_element_type=F32)
        qaug_scr[h, ranked:, own_lanes] = jnp.where(head_masks[h], scaled[ranked:],
                                                    feat[ranked:] + sel_feat).astype(BF16)

    def logits_of(i):
        rows = slice(i * blk, (i + 1) * blk)
        q_aug = jnp.concatenate([qaug_scr[0, rows, :], qaug_scr[1, rows, :]], axis=0)
        mx = None
        for n in range(i + 1):
            tile = lax.dot_general(q_aug, kaug_scr[n * blk:(n + 1) * blk, :], NT_DIMS,
                                   preferred_element_type=F32)
            if n == i:
                tile = jnp.where(causal, tile, NEG_INF)
            logit_scr[i % 2, n] = tile
            halves = jnp.maximum(tile[:, :LANES], tile[:, LANES:])
            mx = halves if mx is None else jnp.maximum(mx, halves)
        return jnp.max(mx, axis=1, keepdims=True)

    m_next = logits_of(0)
    for i in range(nb):
        rows = slice(i * blk, (i + 1) * blk)
        m = m_next
        if i + 1 < nb:
            m_next = logits_of(i + 1)
        acc = [jnp.zeros((blk, 2 * LANES), F32) for _ in range(2)]
        for n in range(i + 1):
            p = jnp.exp(logit_scr[i % 2, n] - m).astype(BF16)
            acc = [acc[h] + jnp.dot(p[h * blk:(h + 1) * blk], vaug_scr[n * blk:(n + 1) * blk, :],
                                    preferred_element_type=F32) for h in range(2)]
        acc = jnp.concatenate(acc, axis=0)
        normed = acc[:, :LANES] / acc[:, LANES:]
        out = jnp.where(head_masks[0], normed[:blk], normed[blk:])
        o_ref[0, rows, :] = (out * _silu(z_ref[0, rows, :])).astype(o_ref.dtype)


def _moba(slopes, proj_a, batch, seq):
    npair = MOBA_WIDTH // LANES
    proj_a = proj_a.reshape(batch, seq, 4 * MOBA_WIDTH)
    nb = seq // MOBA_BLOCK
    assert nb <= MOBA_MAX_BLOCKS, "block-indicator feature lanes"
    kfeat, qfeat = _moba_features(seq)

    def col_block(base):
        return pl.BlockSpec((1, seq, LANES), lambda b, p, *_: (b, 0, base + p))

    return pl.pallas_call(
        _moba_kernel,
        grid_spec=pltpu.PrefetchScalarGridSpec(
            num_scalar_prefetch=1,
            grid=(batch, npair),
            in_specs=[col_block(0), col_block(npair), col_block(2 * npair), col_block(3 * npair),
                      pl.BlockSpec(kfeat.shape, lambda b, p, *_: (0, 0, 0)),
                      pl.BlockSpec(qfeat.shape, lambda b, p, *_: (0, 0, 0))],
            out_specs=pl.BlockSpec((1, seq, LANES), lambda b, p, *_: (b, 0, p)),
            scratch_shapes=[pltpu.VMEM((seq, 2 * LANES), BF16),
                            pltpu.VMEM((2, seq, 2 * LANES), BF16),
                            pltpu.VMEM((seq, 2 * LANES), BF16),
                            pltpu.VMEM((2, nb, 2 * MOBA_BLOCK, MOBA_BLOCK), F32)],
        ),
        out_shape=jax.ShapeDtypeStruct((batch, seq, MOBA_WIDTH), BF16),
        compiler_params=pltpu.CompilerParams(
            dimension_semantics=("arbitrary", "arbitrary"), vmem_limit_bytes=VMEM_LIMIT_BYTES),
        name="moba",
    )(slopes, proj_a, proj_a, proj_a, proj_a, kfeat, qfeat)


def _each(fn, *lists):
    return [fn(*args) for args in zip(*lists)]


def _unit_lower_inverse(a_list, row, col, chunk):
    size = a_list[0].shape[0]
    assert chunk in (PACK_BLOCK, 2 * PACK_BLOCK) and size % (2 * PACK_BLOCK) == 0

    def pack(m, b):
        out = m[0:b]
        for r in range(b, size, b):
            out = out + m[r:r + b]
        return out

    def unpack(pk, b):
        tiled = jnp.concatenate([pk] * (size // b), axis=0)
        return jnp.where(jnp.right_shift(row, b.bit_length() - 1) == jnp.right_shift(col, b.bit_length() - 1),
                         tiled, 0.0)

    b = PACK_BLOCK
    prow = lax.broadcasted_iota(jnp.int32, (b, size), 0)
    pcol = jnp.bitwise_and(lax.broadcasted_iota(jnp.int32, (b, size), 1), b - 1)

    def same_block(shift):
        return jnp.right_shift(prow, shift) == jnp.right_shift(pcol, shift)

    in_pack = jnp.right_shift(row, b.bit_length() - 1) == jnp.right_shift(col, b.bit_length() - 1)
    a_p = _each(lambda a: pack(jnp.where(in_pack, a, 0.0), b), a_list)
    eye_p = jnp.where(prow == pcol, 1.0, 0.0)
    inner = same_block(4)
    d_p = _each(lambda ap: jnp.where(inner, ap, 0.0), a_p)
    x_p = _each(lambda dp: eye_p - dp, d_p)
    p_bd = _each(lambda dp: unpack(dp, b), d_p)
    p_p = _each(_dot, d_p, p_bd)
    for step in range(3):
        p_bd = _each(lambda pp: unpack(pp, b), p_p)
        x_p = _each(lambda xp, pb: xp + _dot(xp, pb), x_p, p_bd)
        if step < 2:
            p_p = _each(_dot, p_p, p_bd)
    shift = 5
    while (1 << shift) <= b:
        outer = same_block(shift)
        coupling_mask = outer & jnp.logical_not(inner)
        x_bd = _each(lambda xp: unpack(xp, b), x_p)
        lx_p = _each(lambda ap, xb: _dot(jnp.where(coupling_mask, ap, 0.0), xb), a_p, x_bd)
        x_p = _each(lambda xp, lp: xp - _dot(xp, unpack(lp, b)), x_p, lx_p)
        inner = outer
        shift += 1
    x_bd = _each(lambda xp: unpack(xp, b), x_p)
    if chunk == b:
        return x_bd
    b2 = 2 * b
    in_pack2 = jnp.right_shift(row, b2.bit_length() - 1) == jnp.right_shift(col, b2.bit_length() - 1)
    x_p2 = _each(lambda xb: pack(xb, b2), x_bd)
    l_p2 = _each(lambda a: pack(jnp.where(in_pack2 & jnp.logical_not(in_pack), a, 0.0), b2), a_list)
    lx_p2 = _each(_dot, l_p2, x_bd)
    x_p2 = _each(lambda xp, lp: xp - _dot(xp, unpack(lp, b2)), x_p2, lx_p2)
    return _each(lambda xp: unpack(xp, b2), x_p2)


def _gdn_kernel(q_ref, k_ref, v_ref, zgate_ref, ba_ref, gnw_ref, o_ref,
                gate_scr, u_scr, lhs_scr, n_scr, egl_scr, intra_scr):
    seq = q_ref.shape[1]
    sup = GDN_SUPER
    chunk = GDN_CHUNK
    per_sup = sup // chunk
    nsup = seq // sup
    hps = GDN_HEADS_PER_STEP
    lhs_rows = 2 * chunk + GDN_HEAD_DIM
    group = pl.program_id(1)
    lane = lax.broadcasted_iota(jnp.int32, (1, LANES), 1)
    row = lax.broadcasted_iota(jnp.int32, (sup, sup), 0)
    col = lax.broadcasted_iota(jnp.int32, (sup, sup), 1)
    chunk_shift = chunk.bit_length() - 1
    same_chunk = jnp.right_shift(row, chunk_shift) == jnp.right_shift(col, chunk_shift)
    causal = same_chunk & (row >= col)
    strict = same_chunk & (row > col)

    @pl.when(group == 0)
    def _():
        tri = jnp.where(causal, 1.0, 0.0)
        for s in range(nsup):
            rows = slice(s * sup, (s + 1) * sup)
            gates = ba_ref[0, rows, :]
            gc = jnp.dot(tri, gates, precision=lax.Precision.HIGHEST, preferred_element_type=F32)
            gate_scr[rows, :] = jnp.where(lane < GDN_HEADS, gates, gc)

    tiles_per_step = GDN_PREP_GROUP // hps

    def prepare(step, carry):
        loaded = []
        for t in range(tiles_per_step):
            rows = pl.ds(pl.multiple_of((step * tiles_per_step + t) * sup, sup), sup)
            for j in range(hps):
                cols = slice(j * LANES, (j + 1) * LANES)
                loaded.append((group * hps + j, gate_scr[rows, :], q_ref[0, rows, cols], k_ref[0, rows, cols],
                               v_ref[0, rows, cols]))
        results = prepare_tiles(*zip(*loaded))
        for t in range(tiles_per_step):
            for j in range(hps):
                store_tile(j, step * tiles_per_step + t, *[r[t * hps + j] for r in results])
        return carry

    def decay_terms(head, gates):
        beta = jnp.sum(jnp.where(lane == head, gates, 0.0), axis=1, keepdims=True)
        gc = jnp.sum(jnp.where(lane == GDN_HEADS + head, gates, 0.0), axis=1, keepdims=True)
        gc_b = jnp.broadcast_to(gc, (sup, LANES))
        gc_row = jnp.concatenate([gc_b[0:LANES].T, gc_b[LANES:].T], axis=1)[0:1, :]
        decay = jnp.exp(jnp.where(causal, gc - gc_row, NEG_INF))
        gl_b = jnp.concatenate(
            [jnp.broadcast_to(gc_b[(c + 1) * chunk - 1:(c + 1) * chunk, :], (chunk, LANES))
             for c in range(per_sup)], axis=0)
        return beta, decay, jnp.exp(gc), jnp.exp(gl_b - gc_b), jnp.exp(gl_b)

    def state_terms(k_dec, w, u, q_dec, egl):
        lhs, ns, egls = [], [], []
        for c in range(per_sup):
            cr = slice(c * chunk, (c + 1) * chunk)
            pn = lax.dot_general(k_dec[cr].astype(BF16),
                                 jnp.concatenate([w[cr], u[cr]], axis=1).astype(BF16),
                                 TN_DIMS, preferred_element_type=F32)
            lhs.append(jnp.concatenate([w[cr], q_dec[cr], pn[:, :GDN_HEAD_DIM]], axis=0).astype(BF16))
            ns.append(pn[:, GDN_HEAD_DIM:])
            egls.append(egl[c * chunk:c * chunk + 8])
        return lhs, ns, egls

    def prepare_tiles(heads, gates, qb, kb, vb):
        beta, decay, egc, e_tail, egl = zip(*_each(decay_terms, heads, gates))
        k_beta = _each(lambda k, b: k * b, kb, beta)
        a = _each(lambda kbt, k, dc: jnp.where(strict, _dot_nt(kbt, k) * dc, 0.0), k_beta, kb, decay)
        t_inv = _unit_lower_inverse(a, row, col, chunk)
        wu = _each(lambda t, kbt, e, v, b: _dot(t, jnp.concatenate([kbt * e, v * b], axis=1)),
                   t_inv, k_beta, egc, vb, beta)
        w = [x[:, :GDN_HEAD_DIM] for x in wu]
        u = [x[:, GDN_HEAD_DIM:] for x in wu]
        intra = _each(lambda q, k, dc: jnp.where(causal, _dot_nt(q, k) * dc, 0.0).astype(BF16),
                      qb, kb, decay)
        q_dec = _each(lambda q, e: q * e, qb, egc)
        k_dec = _each(lambda k, e: k * e, kb, e_tail)
        lhs, ns, egls = zip(*_each(state_terms, k_dec, w, u, q_dec, egl))
        return u, intra, lhs, ns, egls

    def chunk_slot(s, c, size):
        return pl.ds(pl.multiple_of((s * per_sup + c) * size, size), size)

    def store_tile(j, s, u, intra, lhs, ns, egls):
        rows = pl.ds(pl.multiple_of(s * sup, sup), sup)
        u_scr[j, rows, :] = u
        intra_scr[j, rows, :] = intra
        for c in range(per_sup):
            lhs_scr[j, chunk_slot(s, c, lhs_rows), :] = lhs[c]
            n_scr[j, chunk_slot(s, c, GDN_HEAD_DIM), :] = ns[c]
            egl_scr[j, chunk_slot(s, c, 8), :] = egls[c]

    lax.fori_loop(0, nsup // tiles_per_step, prepare, 0)

    gnw = gnw_ref[...]

    def scan(s, states):
        r0 = pl.multiple_of(s * sup, sup)
        states = list(states)
        v_parts = [[] for _ in range(hps)]
        for c in range(per_sup):
            rows = pl.ds(r0 + c * chunk, chunk)
            r = [jnp.dot(lhs_scr[j, chunk_slot(s, c, lhs_rows), :], states[j].astype(BF16),
                         preferred_element_type=F32) for j in range(hps)]
            outs = []
            for j in range(hps):
                v_parts[j].append(u_scr[j, rows, :] - r[j][:chunk])
                v_sup = jnp.concatenate(
                    v_parts[j] + [jnp.zeros(((per_sup - 1 - c) * chunk, LANES), F32)] * (c < per_sup - 1),
                    axis=0)
                o = r[j][chunk:2 * chunk] + jnp.dot(intra_scr[j, rows, :], v_sup.astype(BF16),
                                                    preferred_element_type=F32)
                egl = jnp.broadcast_to(egl_scr[j, chunk_slot(s, c, 8), :][0:1, :], (GDN_HEAD_DIM, LANES))
                states[j] = states[j] * egl - r[j][2 * chunk:] + n_scr[j, chunk_slot(s, c, GDN_HEAD_DIM), :]
                outs.append(o * lax.rsqrt(jnp.mean(o * o, axis=-1, keepdims=True) + RMS_EPS) * gnw)
            o_ref[0, rows, :] = (jnp.concatenate(outs, axis=1) * zgate_ref[0, rows, :]).astype(o_ref.dtype)
        return tuple(states)

    lax.fori_loop(0, nsup, scan, tuple(jnp.zeros((GDN_HEAD_DIM, GDN_HEAD_DIM), F32) for _ in range(hps)))


def _gdn(act_d, gates, gdn_norm_w, batch, seq):
    hps = GDN_HEADS_PER_STEP
    ngroup = GDN_HEADS // hps
    width = hps * GDN_HEAD_DIM
    assert GDN_PREP_GROUP % hps == 0 and (seq // GDN_SUPER) % (GDN_PREP_GROUP // hps) == 0
    act_d = act_d.reshape(batch, seq, 4 * GDN_WIDTH)
    gates = gates.reshape(batch, seq, LANES)

    def col_block(base):
        return pl.BlockSpec((1, seq, width), lambda b, g: (b, 0, base + g))

    def per_head(rows, lanes, dtype):
        return pltpu.VMEM((hps, rows, lanes), dtype)

    nchunk = seq // GDN_CHUNK
    scratch = [
        pltpu.VMEM((seq, LANES), F32),
        per_head(seq, LANES, F32),
        per_head(nchunk * (2 * GDN_CHUNK + GDN_HEAD_DIM), LANES, BF16),
        per_head(nchunk * GDN_HEAD_DIM, LANES, F32),
        per_head(nchunk * 8, LANES, F32),
        per_head(seq, GDN_SUPER, BF16),
    ]
    return pl.pallas_call(
        _gdn_kernel,
        grid=(batch, ngroup),
        in_specs=[col_block(0), col_block(ngroup), col_block(2 * ngroup), col_block(3 * ngroup),
                  pl.BlockSpec((1, seq, LANES), lambda b, g: (b, 0, 0)),
                  pl.BlockSpec((1, LANES), lambda b, g: (0, 0))],
        out_specs=pl.BlockSpec((1, seq, width), lambda b, g: (b, 0, g)),
        scratch_shapes=scratch,
        out_shape=jax.ShapeDtypeStruct((batch, seq, GDN_WIDTH), BF16),
        compiler_params=pltpu.CompilerParams(
            dimension_semantics=("arbitrary", "arbitrary"), vmem_limit_bytes=VMEM_LIMIT_BYTES),
        name="gdn",
    )(act_d, act_d, act_d, act_d, gates, gdn_norm_w)


def _outproj_kernel(x_ref, ma_ref, md_ref, wa_ref, wd_ref, nw_ref, o_ref):
    y = (jnp.dot(ma_ref[...], wa_ref[...], preferred_element_type=F32)
         + jnp.dot(md_ref[...], wd_ref[...], preferred_element_type=F32))
    y = y * lax.rsqrt(jnp.mean(y * y, axis=-1, keepdims=True) + RMS_EPS) * nw_ref[...]
    o_ref[...] = x_ref[...] + y


def _outproj(x2, mix_a, mix_d, w_a, w_d, norm_w, tm):
    n = x2.shape[0]
    return pl.pallas_call(
        _outproj_kernel,
        grid=(n // tm,),
        in_specs=[
            pl.BlockSpec((tm, D_MODEL), lambda i: (i, 0)),
            pl.BlockSpec((tm, MOBA_WIDTH), lambda i: (i, 0)),
            pl.BlockSpec((tm, GDN_WIDTH), lambda i: (i, 0)),
            pl.BlockSpec(w_a.shape, lambda i: (0, 0)),
            pl.BlockSpec(w_d.shape, lambda i: (0, 0)),
            pl.BlockSpec((1, D_MODEL), lambda i: (0, 0)),
        ],
        out_specs=pl.BlockSpec((tm, D_MODEL), lambda i: (i, 0)),
        out_shape=jax.ShapeDtypeStruct((n, D_MODEL), F32),
        compiler_params=pltpu.CompilerParams(
            dimension_semantics=("arbitrary",), vmem_limit_bytes=VMEM_LIMIT_BYTES),
        name="outproj",
    )(x2, mix_a, mix_d, w_a, w_d, norm_w)


def _alibi_slopes():
    start = 2.0 ** (-8.0 / MOBA_HEADS)
    slopes = np.asarray(start ** np.arange(1, MOBA_HEADS + 1), dtype=np.float32)
    return jnp.asarray(slopes)


def _layer(x, norm_pre_w, w_in, conv_w, a_log, dt_bias, gdn_norm_w, w_out, norm_post_w):
    batch, seq, _ = x.shape
    assert seq % MOBA_BLOCK == 0 and seq % GDN_SUPER == 0
    n = batch * seq
    tm = 512
    assert n % tm == 0
    x2 = x.reshape(n, D_MODEL)
    n_main = 4 * MOBA_WIDTH + 4 * GDN_WIDTH
    w_main = w_in[:, :n_main].astype(BF16)
    w_tail = jnp.pad(w_in[:, n_main:], ((0, 0), (0, LANES - 2 * GDN_HEADS))).astype(BF16)
    to_decay_lanes = (GDN_HEADS, LANES - 2 * GDN_HEADS)
    a_log_lanes = jnp.pad(a_log.astype(F32), to_decay_lanes).reshape(1, LANES)
    dt_bias_lanes = jnp.pad(dt_bias.astype(F32), to_decay_lanes).reshape(1, LANES)
    proj_a, act_d, gates = _inproj(x2, norm_pre_w.reshape(1, D_MODEL), w_main, w_tail, conv_w.astype(F32),
                                   a_log_lanes, dt_bias_lanes, tm, seq)
    mix_a = _moba(_alibi_slopes(), proj_a, batch, seq)
    mix_d = _gdn(act_d, gates, gdn_norm_w.reshape(1, GDN_HEAD_DIM), batch, seq)
    w_o = w_out.astype(BF16)
    out = _outproj(x2, mix_a.reshape(n, MOBA_WIDTH), mix_d.reshape(n, GDN_WIDTH),
                   w_o[:MOBA_WIDTH], w_o[MOBA_WIDTH:], norm_post_w.reshape(1, D_MODEL), tm)
    return out.reshape(batch, seq, D_MODEL)


def kernel(x, norm_pre_w, w_in, conv_w, a_log, dt_bias, gdn_norm_w, w_out, norm_post_w):
    for layer in range(norm_pre_w.shape[0]):
        x = _layer(x, norm_pre_w[layer], w_in[layer], conv_w[layer], a_log[layer], dt_bias[layer],
                   gdn_norm_w[layer], w_out[layer], norm_post_w[layer])
    return x
```

```python
import functools

import numpy as np
import jax
import jax.numpy as jnp
from jax import lax
from jax.experimental import pallas as pl
from jax.experimental.pallas import tpu as pltpu

F32 = jnp.float32
BF16 = jnp.bfloat16

D_MODEL = 1024
MOBA_WIDTH = 512
MOBA_HEAD_DIM = 64
MOBA_HEADS = MOBA_WIDTH // MOBA_HEAD_DIM
MOBA_BLOCK = 256
MOBA_TOPK = 3
GDN_WIDTH = 512
GDN_HEAD_DIM = 128
GDN_HEADS = GDN_WIDTH // GDN_HEAD_DIM
GDN_CONV = 4
GDN_CHUNK = 128
GDN_SUPER = 256
GDN_HEADS_PER_STEP = 4
PACK_BLOCK = 64
GDN_PREP_GROUP = 8
RMS_EPS = 1e-6
NEG_INF = -1e30

INPROJ_GROUP = 512
LANES = 128
VMEM_LIMIT_BYTES = 48 * 1024 * 1024
GDN_VMEM_LIMIT_BYTES = 56 * 1024 * 1024

NT_DIMS = (((1,), (1,)), ((), ()))
TN_DIMS = (((0,), (0,)), ((), ()))


def _dot(a, b):
    return jnp.dot(a.astype(BF16), b.astype(BF16), preferred_element_type=F32)


def _dot_nt(a, b):
    return lax.dot_general(a.astype(BF16), b.astype(BF16), NT_DIMS, preferred_element_type=F32)


def _silu(x):
    return x / (1.0 + jnp.exp(-x))


def _l2_normalize(x):
    return x * lax.rsqrt(jnp.sum(x * x, axis=-1, keepdims=True) + RMS_EPS)


def _inproj_kernel(tiles_per_seq, x_ref, nw_ref, wm_ref, wt_ref, cw_ref, alog_ref, dtb_ref,
                   moba_ref, gdn_ref, gate_ref, pad_scr):
    tm = x_ref.shape[0]
    x = x_ref[...]
    h = x * lax.rsqrt(jnp.mean(x * x, axis=-1, keepdims=True) + RMS_EPS) * nw_ref[...]
    hb = h.astype(BF16)
    half = moba_ref.shape[-1]

    conv_width = pad_scr.shape[1]
    first = (pl.program_id(0) % tiles_per_seq) == 0

    @pl.when(first)
    def _():
        pad_scr[0:8, :] = jnp.zeros((8, conv_width), F32)

    @pl.when(jnp.logical_not(first))
    def _():
        pad_scr[0:8, :] = pad_scr[tm:tm + 8, :]

    gw = INPROJ_GROUP
    for g in range(conv_width // gw):
        cols = slice(g * gw, (g + 1) * gw)
        part = (g * gw) // GDN_WIDTH
        d = jnp.dot(hb, wm_ref[:, half + g * gw:half + (g + 1) * gw], preferred_element_type=F32)
        pad_scr[8:, cols] = d
        acc = d * cw_ref[GDN_CONV - 1:GDN_CONV, cols]
        for j in range(GDN_CONV - 1):
            shift = GDN_CONV - 1 - j
            acc = acc + pad_scr[8 - shift:8 - shift + tm, cols] * cw_ref[j:j + 1, cols]
        act = _silu(acc)
        for hh in range(gw // GDN_HEAD_DIM):
            head_cols = slice(hh * GDN_HEAD_DIM, (hh + 1) * GDN_HEAD_DIM)
            out_cols = slice(g * gw + hh * GDN_HEAD_DIM, g * gw + (hh + 1) * GDN_HEAD_DIM)
            if part == 0:
                head_act = _l2_normalize(act[:, head_cols]) * (GDN_HEAD_DIM ** -0.5)
            elif part == 1:
                head_act = _l2_normalize(act[:, head_cols])
            else:
                head_act = act[:, head_cols]
            gdn_ref[:, out_cols] = head_act.astype(gdn_ref.dtype)
        moba_ref[:, cols] = jnp.dot(hb, wm_ref[:, cols], preferred_element_type=F32)
    z = jnp.dot(hb, wm_ref[:, half + conv_width:], preferred_element_type=F32)
    gdn_ref[:, conv_width:] = _silu(z).astype(gdn_ref.dtype)
    ba = jnp.dot(hb, wt_ref[...], preferred_element_type=F32)
    moba_ref[:, conv_width:] = jnp.dot(hb, wm_ref[:, conv_width:half], preferred_element_type=F32)

    lane = lax.broadcasted_iota(jnp.int32, (1, LANES), 1)
    pre = ba + dtb_ref[...]
    softplus = jnp.maximum(pre, 0.0) + jnp.log(1.0 + jnp.exp(-jnp.abs(pre)))
    gate_ref[...] = jnp.where(lane < GDN_HEADS, 1.0 / (1.0 + jnp.exp(-ba)),
                              -jnp.exp(alog_ref[...]) * softplus)


def _inproj(x2, norm_w, w_main, w_tail, conv_w, a_log_lanes, dt_bias_lanes, tm, seq):
    n = x2.shape[0]
    half = w_main.shape[1] // 2
    assert seq % tm == 0 and tm % 8 == 0
    return pl.pallas_call(
        functools.partial(_inproj_kernel, seq // tm),
        grid=(n // tm,),
        in_specs=[
            pl.BlockSpec((tm, D_MODEL), lambda i: (i, 0)),
            pl.BlockSpec((1, D_MODEL), lambda i: (0, 0)),
            pl.BlockSpec(w_main.shape, lambda i: (0, 0)),
            pl.BlockSpec(w_tail.shape, lambda i: (0, 0)),
            pl.BlockSpec(conv_w.shape, lambda i: (0, 0)),
            pl.BlockSpec((1, LANES), lambda i: (0, 0)),
            pl.BlockSpec((1, LANES), lambda i: (0, 0)),
        ],
        out_specs=[
            pl.BlockSpec((tm, half), lambda i: (i, 0)),
            pl.BlockSpec((tm, half), lambda i: (i, 0)),
            pl.BlockSpec((tm, LANES), lambda i: (i, 0)),
        ],
        out_shape=[
            jax.ShapeDtypeStruct((n, half), F32),
            jax.ShapeDtypeStruct((n, half), BF16),
            jax.ShapeDtypeStruct((n, LANES), F32),
        ],
        scratch_shapes=[pltpu.VMEM((tm + 8, conv_w.shape[1]), F32)],
        compiler_params=pltpu.CompilerParams(
            dimension_semantics=("arbitrary",), vmem_limit_bytes=VMEM_LIMIT_BYTES),
        name="inproj",
    )(x2, norm_w, w_main, w_tail, conv_w, a_log_lanes, dt_bias_lanes)


MOBA_MAX_BLOCKS = 8
FEAT_POS_HI = MOBA_MAX_BLOCKS
FEAT_POS_LO = MOBA_MAX_BLOCKS + 1
FEAT_ONE_HI = MOBA_MAX_BLOCKS + 2
FEAT_ONE_LO = MOBA_MAX_BLOCKS + 3


def _moba_features(seq):
    blk = MOBA_BLOCK
    pos = np.arange(seq)
    kfeat = np.zeros((2, seq, LANES), np.float32)
    qfeat = np.zeros((2, seq, LANES), np.float32)
    for h in range(2):
        base = MOBA_HEAD_DIM * (1 - h)
        kfeat[h, pos, base + pos // blk] = 1.0
        kfeat[h, :, base + FEAT_POS_HI] = blk * (pos // blk)
        kfeat[h, :, base + FEAT_POS_LO] = pos % blk
        kfeat[h, :, base + FEAT_ONE_HI] = 1.0
        kfeat[h, :, base + FEAT_ONE_LO] = 1.0
        qfeat[h, :, base + FEAT_POS_HI] = 1.0
        qfeat[h, :, base + FEAT_POS_LO] = 1.0
        qfeat[h, :, base + FEAT_ONE_HI] = -blk * (pos // blk)
        qfeat[h, :, base + FEAT_ONE_LO] = -(pos % blk)
    return jnp.asarray(kfeat), jnp.asarray(qfeat)


def _moba_kernel(slopes_ref, q_ref, k_ref, v_ref, z_ref, kfeat_ref, qfeat_ref, o_ref,
                 kaug_scr, qaug_scr, vaug_scr, logit_scr):
    seq = q_ref.shape[1]
    blk = MOBA_BLOCK
    nb = seq // blk
    pair = pl.program_id(1)
    lane = lax.broadcasted_iota(jnp.int32, (1, LANES), 1)
    head_masks = [(lane >= MOBA_HEAD_DIM * h) & (lane < MOBA_HEAD_DIM * (h + 1)) for h in range(2)]
    feat_lane = [lane - MOBA_HEAD_DIM * (1 - h) for h in range(2)]
    slopes = [slopes_ref[pair * 2 + h] for h in range(2)]

    k_all = k_ref[0]
    vaug_scr[:, :LANES] = v_ref[0].astype(BF16)
    vaug_scr[:, LANES:] = jnp.ones((seq, LANES), BF16)
    row8 = lax.broadcasted_iota(jnp.int32, (MOBA_MAX_BLOCKS, LANES), 0)
    kbar = jnp.zeros((MOBA_MAX_BLOCKS, LANES), F32)
    for n in range(nb):
        kbar = jnp.where(row8 == n, jnp.mean(k_all[n * blk:(n + 1) * blk], axis=0, keepdims=True), kbar)
    to_feat_lanes = [jnp.where(feat_lane[h] == row8, 1.0, 0.0) for h in range(2)]
    for h in range(2):
        kaug_scr[:, h * LANES:(h + 1) * LANES] = jnp.where(head_masks[h], k_all, kfeat_ref[h]).astype(BF16)
        qaug_scr[h, :, (1 - h) * LANES:(2 - h) * LANES] = jnp.zeros((seq, LANES), BF16)
    row = lax.broadcasted_iota(jnp.int32, (2 * blk, blk), 0)
    col = lax.broadcasted_iota(jnp.int32, (2 * blk, blk), 1)
    causal = jnp.bitwise_and(row, blk - 1) >= col

    q_all = q_ref[0]
    ranked = (MOBA_TOPK + 1) * blk
    block_id = lax.broadcasted_iota(jnp.int32, (MOBA_MAX_BLOCKS, seq - ranked), 0)
    own = jnp.right_shift(lax.broadcasted_iota(jnp.int32, (MOBA_MAX_BLOCKS, seq - ranked), 1) + ranked,
                          blk.bit_length() - 1)
    gates = [lax.dot_general(kbar, jnp.where(head_masks[h], q_all[ranked:], 0.0), NT_DIMS,
                             precision=lax.Precision.HIGHEST, preferred_element_type=F32)
             for h in range(2)]
    cnt = [jnp.zeros((MOBA_MAX_BLOCKS, seq - ranked), F32) for _ in range(2)]
    for other in range(nb - 1):
        for h in range(2):
            g_other = gates[h][other:other + 1, :]
            before = (g_other > gates[h]) | ((g_other == gates[h]) & (block_id > other))
            cnt[h] = cnt[h] + jnp.where(before & (own > other), 1.0, 0.0)
    for h in range(2):
        dropped = jnp.where((cnt[h] >= MOBA_TOPK) & (block_id < own), NEG_INF, 0.0)
        feat = slopes[h] * qfeat_ref[h]
        scaled = q_all * (MOBA_HEAD_DIM ** -0.5)
        own_lanes = slice(h * LANES, (h + 1) * LANES)
        qaug_scr[h, :ranked, own_lanes] = jnp.where(head_masks[h], scaled[:ranked], feat[:ranked]).astype(BF16)
        sel_feat = lax.dot_general(dropped, to_feat_lanes[h], TN_DIMS, preferred_element_type=F32)
        qaug_scr[h, ranked:, own_lanes] = jnp.where(head_masks[h], scaled[ranked:],
                                                    feat[ranked:] + sel_feat).astype(BF16)

    def logits_of(i):
        rows = slice(i * blk, (i + 1) * blk)
        q_aug = jnp.concatenate([qaug_scr[0, rows, :], qaug_scr[1, rows, :]], axis=0)
        mx = None
        for n in range(i + 1):
            tile = lax.dot_general(q_aug, kaug_scr[n * blk:(n + 1) * blk, :], NT_DIMS,
                                   preferred_element_type=F32)
            if n == i:
                tile = jnp.where(causal, tile, NEG_INF)
            logit_scr[i % 2, n] = tile
            halves = jnp.maximum(tile[:, :LANES], tile[:, LANES:])
            mx = halves if mx is None else jnp.maximum(mx, halves)
        return jnp.max(mx, axis=1, keepdims=True)

    m_next = logits_of(0)
    for i in range(nb):
        rows = slice(i * blk, (i + 1) * blk)
        m = m_next
        if i + 1 < nb:
            m_next = logits_of(i + 1)
        acc = [jnp.zeros((blk, 2 * LANES), F32) for _ in range(2)]
        for n in range(i + 1):
            p = jnp.exp(logit_scr[i % 2, n] - m).astype(BF16)
            acc = [acc[h] + jnp.dot(p[h * blk:(h + 1) * blk], vaug_scr[n * blk:(n + 1) * blk, :],
                                    preferred_element_type=F32) for h in range(2)]
        acc = jnp.concatenate(acc, axis=0)
        normed = acc[:, :LANES] / acc[:, LANES:]
        out = jnp.where(head_masks[0], normed[:blk], normed[blk:])
        o_ref[0, rows, :] = (out * _silu(z_ref[0, rows, :])).astype(o_ref.dtype)


def _moba(slopes, proj_a, batch, seq):
    npair = MOBA_WIDTH // LANES
    proj_a = proj_a.reshape(batch, seq, 4 * MOBA_WIDTH)
    nb = seq // MOBA_BLOCK
    assert nb <= MOBA_MAX_BLOCKS, "block-indicator feature lanes"
    kfeat, qfeat = _moba_features(seq)

    def col_block(base):
        return pl.BlockSpec((1, seq, LANES), lambda b, p, *_: (b, 0, base + p))

    return pl.pallas_call(
        _moba_kernel,
        grid_spec=pltpu.PrefetchScalarGridSpec(
            num_scalar_prefetch=1,
            grid=(batch, npair),
            in_specs=[col_block(0), col_block(npair), col_block(2 * npair), col_block(3 * npair),
                      pl.BlockSpec(kfeat.shape, lambda b, p, *_: (0, 0, 0)),
                      pl.BlockSpec(qfeat.shape, lambda b, p, *_: (0, 0, 0))],
            out_specs=pl.BlockSpec((1, seq, LANES), lambda b, p, *_: (b, 0, p)),
            scratch_shapes=[pltpu.VMEM((seq, 2 * LANES), BF16),
                            pltpu.VMEM((2, seq, 2 * LANES), BF16),
                            pltpu.VMEM((seq, 2 * LANES), BF16),
                            pltpu.VMEM((2, nb, 2 * MOBA_BLOCK, MOBA_BLOCK), F32)],
        ),
        out_shape=jax.ShapeDtypeStruct((batch, seq, MOBA_WIDTH), BF16),
        compiler_params=pltpu.CompilerParams(
            dimension_semantics=("arbitrary", "arbitrary"), vmem_limit_bytes=VMEM_LIMIT_BYTES),
        name="moba",
    )(slopes, proj_a, proj_a, proj_a, proj_a, kfeat, qfeat)


def _each(fn, *lists):
    return [fn(*args) for args in zip(*lists)]


def _unit_lower_inverse(a_list, row, col, chunk):
    size = a_list[0].shape[0]
    assert chunk in (PACK_BLOCK, 2 * PACK_BLOCK) and size % (2 * PACK_BLOCK) == 0

    def pack(m, b):
        out = m[0:b]
        for r in range(b, size, b):
            out = out + m[r:r + b]
        return out

    def unpack(pk, b):
        tiled = jnp.concatenate([pk] * (size // b), axis=0)
        return jnp.where(jnp.right_shift(row, b.bit_length() - 1) == jnp.right_shift(col, b.bit_length() - 1),
                         tiled, 0.0)

    b = PACK_BLOCK
    prow = lax.broadcasted_iota(jnp.int32, (b, size), 0)
    pcol = jnp.bitwise_and(lax.broadcasted_iota(jnp.int32, (b, size), 1), b - 1)

    def same_block(shift):
        return jnp.right_shift(prow, shift) == jnp.right_shift(pcol, shift)

    in_pack = jnp.right_shift(row, b.bit_length() - 1) == jnp.right_shift(col, b.bit_length() - 1)
    a_p = _each(lambda a: pack(jnp.where(in_pack, a, 0.0), b), a_list)
    eye_p = jnp.where(prow == pcol, 1.0, 0.0)
    inner = same_block(4)
    d_p = _each(lambda ap: jnp.where(inner, ap, 0.0), a_p)
    x_p = _each(lambda dp: eye_p - dp, d_p)
    p_bd = _each(lambda dp: unpack(dp, b), d_p)
    p_p = _each(_dot, d_p, p_bd)
    for step in range(3):
        p_bd = _each(lambda pp: unpack(pp, b), p_p)
        x_p = _each(lambda xp, pb: xp + _dot(xp, pb), x_p, p_bd)
        if step < 2:
            p_p = _each(_dot, p_p, p_bd)
    shift = 5
    while (1 << shift) <= b:
        outer = same_block(shift)
        coupling_mask = outer & jnp.logical_not(inner)
        x_bd = _each(lambda xp: unpack(xp, b), x_p)
        lx_p = _each(lambda ap, xb: _dot(jnp.where(coupling_mask, ap, 0.0), xb), a_p, x_bd)
        x_p = _each(lambda xp, lp: xp - _dot(xp, unpack(lp, b)), x_p, lx_p)
        inner = outer
        shift += 1
    x_bd = _each(lambda xp: unpack(xp, b), x_p)
    if chunk == b:
        return x_bd
    b2 = 2 * b
    in_pack2 = jnp.right_shift(row, b2.bit_length() - 1) == jnp.right_shift(col, b2.bit_length() - 1)
    x_p2 = _each(lambda xb: pack(xb, b2), x_bd)
    l_p2 = _each(lambda a: pack(jnp.where(in_pack2 & jnp.logical_not(in_pack), a, 0.0), b2), a_list)
    lx_p2 = _each(_dot, l_p2, x_bd)
    x_p2 = _each(lambda xp, lp: xp - _dot(xp, unpack(lp, b2)), x_p2, lx_p2)
    return _each(lambda xp: unpack(xp, b2), x_p2)


def _gdn_kernel(q_ref, k_ref, v_ref, zgate_ref, ba_ref, gnw_ref, o_ref,
                gate_scr, u_scr, lhs_scr, n_scr, egl_scr, intra_scr):
    seq = q_ref.shape[1]
    sup = GDN_SUPER
    chunk = GDN_CHUNK
    per_sup = sup // chunk
    nsup = seq // sup
    hps = GDN_HEADS_PER_STEP
    lhs_rows = 2 * chunk + GDN_HEAD_DIM
    group = pl.program_id(1)
    lane = lax.broadcasted_iota(jnp.int32, (1, LANES), 1)
    row = lax.broadcasted_iota(jnp.int32, (sup, sup), 0)
    col = lax.broadcasted_iota(jnp.int32, (sup, sup), 1)
    chunk_shift = chunk.bit_length() - 1
    same_chunk = jnp.right_shift(row, chunk_shift) == jnp.right_shift(col, chunk_shift)
    causal = same_chunk & (row >= col)
    strict = same_chunk & (row > col)

    @pl.when(group == 0)
    def _():
        tri = jnp.where(causal, 1.0, 0.0)
        for s in range(nsup):
            rows = slice(s * sup, (s + 1) * sup)
            gates = ba_ref[0, rows, :]
            gc = jnp.dot(tri, gates, precision=lax.Precision.HIGHEST, preferred_element_type=F32)
            gate_scr[rows, :] = jnp.where(lane < GDN_HEADS, gates, gc)

    tiles_per_step = GDN_PREP_GROUP // hps

    def prepare(step, carry):
        loaded = []
        for t in range(tiles_per_step):
            rows = pl.ds(pl.multiple_of((step * tiles_per_step + t) * sup, sup), sup)
            for j in range(hps):
                cols = slice(j * LANES, (j + 1) * LANES)
                loaded.append((group * hps + j, gate_scr[rows, :], q_ref[0, rows, cols].astype(F32),
                               k_ref[0, rows, cols].astype(F32), v_ref[0, rows, cols].astype(F32)))
        results = prepare_tiles(*zip(*loaded))
        for t in range(tiles_per_step):
            for j in range(hps):
                store_tile(j, step * tiles_per_step + t, *[r[t * hps + j] for r in results])
        return carry

    def decay_terms(head, gates):
        beta = jnp.sum(jnp.where(lane == head, gates, 0.0), axis=1, keepdims=True)
        gc = jnp.sum(jnp.where(lane == GDN_HEADS + head, gates, 0.0), axis=1, keepdims=True)
        gc_b = jnp.broadcast_to(gc, (sup, LANES))
        gc_row = jnp.concatenate([gc_b[0:LANES].T, gc_b[LANES:].T], axis=1)[0:1, :]
        decay = jnp.exp(jnp.where(causal, gc - gc_row, NEG_INF))
        gl_b = jnp.concatenate(
            [jnp.broadcast_to(gc_b[(c + 1) * chunk - 1:(c + 1) * chunk, :], (chunk, LANES))
             for c in range(per_sup)], axis=0)
        return beta, decay, jnp.exp(gc), jnp.exp(gl_b - gc_b), jnp.exp(gl_b)

    def state_terms(k_dec, w, u, q_dec, egl):
        lhs, ns, egls = [], [], []
        for c in range(per_sup):
            cr = slice(c * chunk, (c + 1) * chunk)
            pn = lax.dot_general(k_dec[cr].astype(BF16),
                                 jnp.concatenate([w[cr], u[cr]], axis=1).astype(BF16),
                                 TN_DIMS, preferred_element_type=F32)
            lhs.append(jnp.concatenate([w[cr], q_dec[cr], pn[:, :GDN_HEAD_DIM]], axis=0).astype(BF16))
            ns.append(pn[:, GDN_HEAD_DIM:])
            egls.append(egl[c * chunk:c * chunk + 8])
        return lhs, ns, egls

    def prepare_tiles(heads, gates, qb, kb, vb):
        beta, decay, egc, e_tail, egl = zip(*_each(decay_terms, heads, gates))
        k_beta = _each(lambda k, b: k * b, kb, beta)
        a = _each(lambda kbt, k, dc: jnp.where(strict, _dot_nt(kbt, k) * dc, 0.0), k_beta, kb, decay)
        t_inv = _unit_lower_inverse(a, row, col, chunk)
        wu = _each(lambda t, kbt, e, v, b: _dot(t, jnp.concatenate([kbt * e, v * b], axis=1)),
                   t_inv, k_beta, egc, vb, beta)
        w = [x[:, :GDN_HEAD_DIM] for x in wu]
        u = [x[:, GDN_HEAD_DIM:] for x in wu]
        intra = _each(lambda q, k, dc: jnp.where(causal, _dot_nt(q, k) * dc, 0.0).astype(BF16),
                      qb, kb, decay)
        q_dec = _each(lambda q, e: q * e, qb, egc)
        k_dec = _each(lambda k, e: k * e, kb, e_tail)
        lhs, ns, egls = zip(*_each(state_terms, k_dec, w, u, q_dec, egl))
        return u, intra, lhs, ns, egls

    def chunk_slot(s, c, size):
        return pl.ds(pl.multiple_of((s * per_sup + c) * size, size), size)

    def store_tile(j, s, u, intra, lhs, ns, egls):
        rows = pl.ds(pl.multiple_of(s * sup, sup), sup)
        u_scr[j, rows, :] = u
        intra_scr[j, rows, :] = intra
        for c in range(per_sup):
            lhs_scr[j, chunk_slot(s, c, lhs_rows), :] = lhs[c]
            n_scr[j, chunk_slot(s, c, GDN_HEAD_DIM), :] = ns[c]
            egl_scr[j, chunk_slot(s, c, 8), :] = egls[c]

    lax.fori_loop(0, nsup // tiles_per_step, prepare, 0)

    gnw = gnw_ref[...]

    def scan(s, states):
        r0 = pl.multiple_of(s * sup, sup)
        states = list(states)
        v_parts = [[] for _ in range(hps)]
        for c in range(per_sup):
            rows = pl.ds(r0 + c * chunk, chunk)
            r = [jnp.dot(lhs_scr[j, chunk_slot(s, c, lhs_rows), :], states[j].astype(BF16),
                         preferred_element_type=F32) for j in range(hps)]
            outs = []
            for j in range(hps):
                v_parts[j].append(u_scr[j, rows, :] - r[j][:chunk])
                v_sup = jnp.concatenate(
                    v_parts[j] + [jnp.zeros(((per_sup - 1 - c) * chunk, LANES), F32)] * (c < per_sup - 1),
                    axis=0)
                o = r[j][chunk:2 * chunk] + jnp.dot(intra_scr[j, rows, :], v_sup.astype(BF16),
                                                    preferred_element_type=F32)
                egl = jnp.broadcast_to(egl_scr[j, chunk_slot(s, c, 8), :][0:1, :], (GDN_HEAD_DIM, LANES))
                states[j] = states[j] * egl - r[j][2 * chunk:] + n_scr[j, chunk_slot(s, c, GDN_HEAD_DIM), :]
                outs.append(o * lax.rsqrt(jnp.mean(o * o, axis=-1, keepdims=True) + RMS_EPS) * gnw)
            o_ref[0, rows, :] = (jnp.concatenate(outs, axis=1)
                                 * zgate_ref[0, rows, :].astype(F32)).astype(o_ref.dtype)
        return tuple(states)

    lax.fori_loop(0, nsup, scan, tuple(jnp.zeros((GDN_HEAD_DIM, GDN_HEAD_DIM), F32) for _ in range(hps)))


def _gdn(act_d, gates, gdn_norm_w, batch, seq):
    hps = GDN_HEADS_PER_STEP
    ngroup = GDN_HEADS // hps
    width = hps * GDN_HEAD_DIM
    assert GDN_PREP_GROUP % hps == 0 and (seq // GDN_SUPER) % (GDN_PREP_GROUP // hps) == 0
    act_d = act_d.reshape(batch, seq, 4 * GDN_WIDTH)
    gates = gates.reshape(batch, seq, LANES)

    def col_block(base):
        return pl.BlockSpec((1, seq, width), lambda b, g: (b, 0, base + g))

    def per_head(rows, lanes, dtype):
        return pltpu.VMEM((hps, rows, lanes), dtype)

    nchunk = seq // GDN_CHUNK
    scratch = [
        pltpu.VMEM((seq, LANES), F32),
        per_head(seq, LANES, F32),
        per_head(nchunk * (2 * GDN_CHUNK + GDN_HEAD_DIM), LANES, BF16),
        per_head(nchunk * GDN_HEAD_DIM, LANES, F32),
        per_head(nchunk * 8, LANES, F32),
        per_head(seq, GDN_SUPER, BF16),
    ]
    return pl.pallas_call(
        _gdn_kernel,
        grid=(batch, ngroup),
        in_specs=[col_block(0), col_block(ngroup), col_block(2 * ngroup), col_block(3 * ngroup),
                  pl.BlockSpec((1, seq, LANES), lambda b, g: (b, 0, 0)),
                  pl.BlockSpec((1, LANES), lambda b, g: (0, 0))],
        out_specs=pl.BlockSpec((1, seq, width), lambda b, g: (b, 0, g)),
        scratch_shapes=scratch,
        out_shape=jax.ShapeDtypeStruct((batch, seq, GDN_WIDTH), BF16),
        compiler_params=pltpu.CompilerParams(
            dimension_semantics=("arbitrary", "arbitrary"), vmem_limit_bytes=GDN_VMEM_LIMIT_BYTES),
        name="gdn",
    )(act_d, act_d, act_d, act_d, gates, gdn_norm_w)


def _outproj_kernel(x_ref, ma_ref, md_ref, wa_ref, wd_ref, nw_ref, o_ref):
    y = (jnp.dot(ma_ref[...], wa_ref[...], preferred_element_type=F32)
         + jnp.dot(md_ref[...], wd_ref[...], preferred_element_type=F32))
    y = y * lax.rsqrt(jnp.mean(y * y, axis=-1, keepdims=True) + RMS_EPS) * nw_ref[...]
    o_ref[...] = x_ref[...] + y


def _outproj(x2, mix_a, mix_d, w_a, w_d, norm_w, tm):
    n = x2.shape[0]
    return pl.pallas_call(
        _outproj_kernel,
        grid=(n // tm,),
        in_specs=[
            pl.BlockSpec((tm, D_MODEL), lambda i: (i, 0)),
            pl.BlockSpec((tm, MOBA_WIDTH), lambda i: (i, 0)),
            pl.BlockSpec((tm, GDN_WIDTH), lambda i: (i, 0)),
            pl.BlockSpec(w_a.shape, lambda i: (0, 0)),
            pl.BlockSpec(w_d.shape, lambda i: (0, 0)),
            pl.BlockSpec((1, D_MODEL), lambda i: (0, 0)),
        ],
        out_specs=pl.BlockSpec((tm, D_MODEL), lambda i: (i, 0)),
        out_shape=jax.ShapeDtypeStruct((n, D_MODEL), F32),
        compiler_params=pltpu.CompilerParams(
            dimension_semantics=("arbitrary",), vmem_limit_bytes=VMEM_LIMIT_BYTES),
        name="outproj",
    )(x2, mix_a, mix_d, w_a, w_d, norm_w)


def _alibi_slopes():
    start = 2.0 ** (-8.0 / MOBA_HEADS)
    slopes = np.asarray(start ** np.arange(1, MOBA_HEADS + 1), dtype=np.float32)
    return jnp.asarray(slopes)


def _layer(x, norm_pre_w, w_in, conv_w, a_log, dt_bias, gdn_norm_w, w_out, norm_post_w):
    batch, seq, _ = x.shape
    assert seq % MOBA_BLOCK == 0 and seq % GDN_SUPER == 0
    n = batch * seq
    tm = 512
    assert n % tm == 0
    x2 = x.reshape(n, D_MODEL)
    n_main = 4 * MOBA_WIDTH + 4 * GDN_WIDTH
    w_main = w_in[:, :n_main].astype(BF16)
    w_tail = jnp.pad(w_in[:, n_main:], ((0, 0), (0, LANES - 2 * GDN_HEADS))).astype(BF16)
    to_decay_lanes = (GDN_HEADS, LANES - 2 * GDN_HEADS)
    a_log_lanes = jnp.pad(a_log.astype(F32), to_decay_lanes).reshape(1, LANES)
    dt_bias_lanes = jnp.pad(dt_bias.astype(F32), to_decay_lanes).reshape(1, LANES)
    proj_a, act_d, gates = _inproj(x2, norm_pre_w.reshape(1, D_MODEL), w_main, w_tail, conv_w.astype(F32),
                                   a_log_lanes, dt_bias_lanes, tm, seq)
    mix_a = _moba(_alibi_slopes(), proj_a, batch, seq)
    mix_d = _gdn(act_d, gates, gdn_norm_w.reshape(1, GDN_HEAD_DIM), batch, seq)
    w_o = w_out.astype(BF16)
    out = _outproj(x2, mix_a.reshape(n, MOBA_WIDTH), mix_d.reshape(n, GDN_WIDTH),
                   w_o[:MOBA_WIDTH], w_o[MOBA_WIDTH:], norm_post_w.reshape(1, D_MODEL), tm)
    return out.reshape(batch, seq, D_MODEL)


def kernel(x, norm_pre_w, w_in, conv_w, a_log, dt_bias, gdn_norm_w, w_out, norm_post_w):
    for layer in range(norm_pre_w.shape[0]):
        x = _layer(x, norm_pre_w[layer], w_in[layer], conv_w[layer], a_log[layer], dt_bias[layer],
                   gdn_norm_w[layer], w_out[layer], norm_post_w[layer])
    return x
```

```python
import functools

import numpy as np
import jax
import jax.numpy as jnp
from jax import lax
from jax.experimental import pallas as pl
from jax.experimental.pallas import tpu as pltpu

F32 = jnp.float32
BF16 = jnp.bfloat16

D_MODEL = 1024
MOBA_WIDTH = 512
MOBA_HEAD_DIM = 64
MOBA_HEADS = MOBA_WIDTH // MOBA_HEAD_DIM
MOBA_BLOCK = 256
MOBA_TOPK = 3
GDN_WIDTH = 512
GDN_HEAD_DIM = 128
GDN_HEADS = GDN_WIDTH // GDN_HEAD_DIM
GDN_CONV = 4
GDN_CHUNK = 128
GDN_SUPER = 256
GDN_HEADS_PER_STEP = 4
PACK_BLOCK = 64
GDN_PREP_GROUP = 8
RMS_EPS = 1e-6
NEG_INF = -1e30

INPROJ_GROUP = 512
LANES = 128
VMEM_LIMIT_BYTES = 48 * 1024 * 1024
GDN_VMEM_LIMIT_BYTES = 56 * 1024 * 1024

NT_DIMS = (((1,), (1,)), ((), ()))
TN_DIMS = (((0,), (0,)), ((), ()))


def _dot(a, b):
    return jnp.dot(a.astype(BF16), b.astype(BF16), preferred_element_type=F32)


def _dot_nt(a, b):
    return lax.dot_general(a.astype(BF16), b.astype(BF16), NT_DIMS, preferred_element_type=F32)


def _silu(x):
    return x / (1.0 + jnp.exp(-x))


def _l2_normalize(x):
    return x * lax.rsqrt(jnp.sum(x * x, axis=-1, keepdims=True) + RMS_EPS)


def _inproj_kernel(tiles_per_seq, x_ref, nw_ref, wm_ref, wt_ref, cw_ref, alog_ref, dtb_ref,
                   moba_ref, gdn_ref, gate_ref, pad_scr):
    tm = x_ref.shape[0]
    x = x_ref[...]
    h = x * lax.rsqrt(jnp.mean(x * x, axis=-1, keepdims=True) + RMS_EPS) * nw_ref[...]
    hb = h.astype(BF16)
    half = moba_ref.shape[-1]

    conv_width = pad_scr.shape[1]
    first = (pl.program_id(0) % tiles_per_seq) == 0

    @pl.when(first)
    def _():
        pad_scr[0:8, :] = jnp.zeros((8, conv_width), F32)

    @pl.when(jnp.logical_not(first))
    def _():
        pad_scr[0:8, :] = pad_scr[tm:tm + 8, :]

    gw = INPROJ_GROUP
    for g in range(conv_width // gw):
        cols = slice(g * gw, (g + 1) * gw)
        part = (g * gw) // GDN_WIDTH
        d = jnp.dot(hb, wm_ref[:, half + g * gw:half + (g + 1) * gw], preferred_element_type=F32)
        pad_scr[8:, cols] = d
        acc = d * cw_ref[GDN_CONV - 1:GDN_CONV, cols]
        for j in range(GDN_CONV - 1):
            shift = GDN_CONV - 1 - j
            acc = acc + pad_scr[8 - shift:8 - shift + tm, cols] * cw_ref[j:j + 1, cols]
        act = _silu(acc)
        for hh in range(gw // GDN_HEAD_DIM):
            head_cols = slice(hh * GDN_HEAD_DIM, (hh + 1) * GDN_HEAD_DIM)
            out_cols = slice(g * gw + hh * GDN_HEAD_DIM, g * gw + (hh + 1) * GDN_HEAD_DIM)
            if part == 0:
                head_act = _l2_normalize(act[:, head_cols]) * (GDN_HEAD_DIM ** -0.5)
            elif part == 1:
                head_act = _l2_normalize(act[:, head_cols])
            else:
                head_act = act[:, head_cols]
            gdn_ref[:, out_cols] = head_act.astype(gdn_ref.dtype)
        moba_ref[:, cols] = jnp.dot(hb, wm_ref[:, cols], preferred_element_type=F32)
    z = jnp.dot(hb, wm_ref[:, half + conv_width:], preferred_element_type=F32)
    gdn_ref[:, conv_width:] = _silu(z).astype(gdn_ref.dtype)
    ba = jnp.dot(hb, wt_ref[...], preferred_element_type=F32)
    moba_ref[:, conv_width:] = jnp.dot(hb, wm_ref[:, conv_width:half], preferred_element_type=F32)

    lane = lax.broadcasted_iota(jnp.int32, (1, LANES), 1)
    pre = ba + dtb_ref[...]
    softplus = jnp.maximum(pre, 0.0) + jnp.log(1.0 + jnp.exp(-jnp.abs(pre)))
    gate_ref[...] = jnp.where(lane < GDN_HEADS, 1.0 / (1.0 + jnp.exp(-ba)),
                              -jnp.exp(alog_ref[...]) * softplus)


def _inproj(x2, norm_w, w_main, w_tail, conv_w, a_log_lanes, dt_bias_lanes, tm, seq):
    n = x2.shape[0]
    half = w_main.shape[1] // 2
    assert seq % tm == 0 and tm % 8 == 0
    return pl.pallas_call(
        functools.partial(_inproj_kernel, seq // tm),
        grid=(n // tm,),
        in_specs=[
            pl.BlockSpec((tm, D_MODEL), lambda i: (i, 0)),
            pl.BlockSpec((1, D_MODEL), lambda i: (0, 0)),
            pl.BlockSpec(w_main.shape, lambda i: (0, 0)),
            pl.BlockSpec(w_tail.shape, lambda i: (0, 0)),
            pl.BlockSpec(conv_w.shape, lambda i: (0, 0)),
            pl.BlockSpec((1, LANES), lambda i: (0, 0)),
            pl.BlockSpec((1, LANES), lambda i: (0, 0)),
        ],
        out_specs=[
            pl.BlockSpec((tm, half), lambda i: (i, 0)),
            pl.BlockSpec((tm, half), lambda i: (i, 0)),
            pl.BlockSpec((tm, LANES), lambda i: (i, 0)),
        ],
        out_shape=[
            jax.ShapeDtypeStruct((n, half), F32),
            jax.ShapeDtypeStruct((n, half), BF16),
            jax.ShapeDtypeStruct((n, LANES), F32),
        ],
        scratch_shapes=[pltpu.VMEM((tm + 8, conv_w.shape[1]), F32)],
        compiler_params=pltpu.CompilerParams(
            dimension_semantics=("arbitrary",), vmem_limit_bytes=VMEM_LIMIT_BYTES),
        name="inproj",
    )(x2, norm_w, w_main, w_tail, conv_w, a_log_lanes, dt_bias_lanes)


MOBA_MAX_BLOCKS = 8
FEAT_POS_HI = MOBA_MAX_BLOCKS
FEAT_POS_LO = MOBA_MAX_BLOCKS + 1
FEAT_ONE_HI = MOBA_MAX_BLOCKS + 2
FEAT_ONE_LO = MOBA_MAX_BLOCKS + 3


def _moba_features(seq):
    blk = MOBA_BLOCK
    pos = np.arange(seq)
    kfeat = np.zeros((2, seq, LANES), np.float32)
    qfeat = np.zeros((2, seq, LANES), np.float32)
    for h in range(2):
        base = MOBA_HEAD_DIM * (1 - h)
        kfeat[h, pos, base + pos // blk] = 1.0
        kfeat[h, :, base + FEAT_POS_HI] = blk * (pos // blk)
        kfeat[h, :, base + FEAT_POS_LO] = pos % blk
        kfeat[h, :, base + FEAT_ONE_HI] = 1.0
        kfeat[h, :, base + FEAT_ONE_LO] = 1.0
        qfeat[h, :, base + FEAT_POS_HI] = 1.0
        qfeat[h, :, base + FEAT_POS_LO] = 1.0
        qfeat[h, :, base + FEAT_ONE_HI] = -blk * (pos // blk)
        qfeat[h, :, base + FEAT_ONE_LO] = -(pos % blk)
    return jnp.asarray(kfeat), jnp.asarray(qfeat)


def _moba_kernel(slopes_ref, q_ref, k_ref, v_ref, z_ref, kfeat_ref, qfeat_ref, o_ref,
                 kaug_scr, qaug_scr, vaug_scr, logit_scr):
    seq = q_ref.shape[1]
    blk = MOBA_BLOCK
    nb = seq // blk
    pair = pl.program_id(1)
    lane = lax.broadcasted_iota(jnp.int32, (1, LANES), 1)
    head_masks = [(lane >= MOBA_HEAD_DIM * h) & (lane < MOBA_HEAD_DIM * (h + 1)) for h in range(2)]
    feat_lane = [lane - MOBA_HEAD_DIM * (1 - h) for h in range(2)]
    slopes = [slopes_ref[pair * 2 + h] for h in range(2)]

    k_all = k_ref[0]
    vaug_scr[:, :LANES] = v_ref[0].astype(BF16)
    vaug_scr[:, LANES:] = jnp.ones((seq, LANES), BF16)
    row8 = lax.broadcasted_iota(jnp.int32, (MOBA_MAX_BLOCKS, LANES), 0)
    kbar = jnp.zeros((MOBA_MAX_BLOCKS, LANES), F32)
    for n in range(nb):
        kbar = jnp.where(row8 == n, jnp.mean(k_all[n * blk:(n + 1) * blk], axis=0, keepdims=True), kbar)
    to_feat_lanes = [jnp.where(feat_lane[h] == row8, 1.0, 0.0) for h in range(2)]
    for h in range(2):
        kaug_scr[:, h * LANES:(h + 1) * LANES] = jnp.where(head_masks[h], k_all, kfeat_ref[h]).astype(BF16)
        qaug_scr[h, :, (1 - h) * LANES:(2 - h) * LANES] = jnp.zeros((seq, LANES), BF16)
    row = lax.broadcasted_iota(jnp.int32, (2 * blk, blk), 0)
    col = lax.broadcasted_iota(jnp.int32, (2 * blk, blk), 1)
    causal = jnp.bitwise_and(row, blk - 1) >= col

    q_all = q_ref[0]
    ranked = (MOBA_TOPK + 1) * blk
    block_id = lax.broadcasted_iota(jnp.int32, (MOBA_MAX_BLOCKS, seq - ranked), 0)
    own = jnp.right_shift(lax.broadcasted_iota(jnp.int32, (MOBA_MAX_BLOCKS, seq - ranked), 1) + ranked,
                          blk.bit_length() - 1)
    gates = [lax.dot_general(kbar, jnp.where(head_masks[h], q_all[ranked:], 0.0), NT_DIMS,
                             precision=lax.Precision.HIGHEST, preferred_element_type=F32)
             for h in range(2)]
    cnt = [jnp.zeros((MOBA_MAX_BLOCKS, seq - ranked), F32) for _ in range(2)]
    for other in range(nb - 1):
        for h in range(2):
            g_other = gates[h][other:other + 1, :]
            before = (g_other > gates[h]) | ((g_other == gates[h]) & (block_id > other))
            cnt[h] = cnt[h] + jnp.where(before & (own > other), 1.0, 0.0)
    for h in range(2):
        dropped = jnp.where((cnt[h] >= MOBA_TOPK) & (block_id < own), NEG_INF, 0.0)
        feat = slopes[h] * qfeat_ref[h]
        scaled = q_all * (MOBA_HEAD_DIM ** -0.5)
        own_lanes = slice(h * LANES, (h + 1) * LANES)
        qaug_scr[h, :ranked, own_lanes] = jnp.where(head_masks[h], scaled[:ranked], feat[:ranked]).astype(BF16)
        sel_feat = lax.dot_general(dropped, to_feat_lanes[h], TN_DIMS, preferred_element_type=F32)
        qaug_scr[h, ranked:, own_lanes] = jnp.where(head_masks[h], scaled[ranked:],
                                                    feat[ranked:] + sel_feat).astype(BF16)

    def logits_of(i):
        rows = slice(i * blk, (i + 1) * blk)
        q_aug = jnp.concatenate([qaug_scr[0, rows, :], qaug_scr[1, rows, :]], axis=0)
        mx = None
        for n in range(i + 1):
            tile = lax.dot_general(q_aug, kaug_scr[n * blk:(n + 1) * blk, :], NT_DIMS,
                                   preferred_element_type=F32)
            if n == i:
                tile = jnp.where(causal, tile, NEG_INF)
            logit_scr[i % 2, n] = tile
            halves = jnp.maximum(tile[:, :LANES], tile[:, LANES:])
            mx = halves if mx is None else jnp.maximum(mx, halves)
        return jnp.max(mx, axis=1, keepdims=True)

    m_next = logits_of(0)
    for i in range(nb):
        rows = slice(i * blk, (i + 1) * blk)
        m = m_next
        if i + 1 < nb:
            m_next = logits_of(i + 1)
        acc = [jnp.zeros((blk, 2 * LANES), F32) for _ in range(2)]
        for n in range(i + 1):
            p = jnp.exp(logit_scr[i % 2, n] - m).astype(BF16)
            acc = [acc[h] + jnp.dot(p[h * blk:(h + 1) * blk], vaug_scr[n * blk:(n + 1) * blk, :],
                                    preferred_element_type=F32) for h in range(2)]
        acc = jnp.concatenate(acc, axis=0)
        normed = acc[:, :LANES] / acc[:, LANES:]
        out = jnp.where(head_masks[0], normed[:blk], normed[blk:])
        o_ref[0, rows, :] = (out * _silu(z_ref[0, rows, :])).astype(o_ref.dtype)


def _moba(slopes, proj_a, batch, seq):
    npair = MOBA_WIDTH // LANES
    proj_a = proj_a.reshape(batch, seq, 4 * MOBA_WIDTH)
    nb = seq // MOBA_BLOCK
    assert nb <= MOBA_MAX_BLOCKS, "block-indicator feature lanes"
    kfeat, qfeat = _moba_features(seq)

    def col_block(base):
        return pl.BlockSpec((1, seq, LANES), lambda b, p, *_: (b, 0, base + p))

    return pl.pallas_call(
        _moba_kernel,
        grid_spec=pltpu.PrefetchScalarGridSpec(
            num_scalar_prefetch=1,
            grid=(batch, npair),
            in_specs=[col_block(0), col_block(npair), col_block(2 * npair), col_block(3 * npair),
                      pl.BlockSpec(kfeat.shape, lambda b, p, *_: (0, 0, 0)),
                      pl.BlockSpec(qfeat.shape, lambda b, p, *_: (0, 0, 0))],
            out_specs=pl.BlockSpec((1, seq, LANES), lambda b, p, *_: (b, 0, p)),
            scratch_shapes=[pltpu.VMEM((seq, 2 * LANES), BF16),
                            pltpu.VMEM((2, seq, 2 * LANES), BF16),
                            pltpu.VMEM((seq, 2 * LANES), BF16),
                            pltpu.VMEM((2, nb, 2 * MOBA_BLOCK, MOBA_BLOCK), F32)],
        ),
        out_shape=jax.ShapeDtypeStruct((batch, seq, MOBA_WIDTH), BF16),
        compiler_params=pltpu.CompilerParams(
            dimension_semantics=("arbitrary", "arbitrary"), vmem_limit_bytes=VMEM_LIMIT_BYTES),
        name="moba",
    )(slopes, proj_a, proj_a, proj_a, proj_a, kfeat, qfeat)


def _each(fn, *lists):
    return [fn(*args) for args in zip(*lists)]


def _unit_lower_inverse(a_list, row, col, chunk):
    size = a_list[0].shape[0]
    assert chunk in (PACK_BLOCK, 2 * PACK_BLOCK) and size % (2 * PACK_BLOCK) == 0

    def pack(m, b):
        out = m[0:b]
        for r in range(b, size, b):
            out = out + m[r:r + b]
        return out

    def unpack(pk, b):
        tiled = jnp.concatenate([pk] * (size // b), axis=0)
        return jnp.where(jnp.right_shift(row, b.bit_length() - 1) == jnp.right_shift(col, b.bit_length() - 1),
                         tiled, 0.0)

    b = PACK_BLOCK
    prow = lax.broadcasted_iota(jnp.int32, (b, size), 0)
    pcol = jnp.bitwise_and(lax.broadcasted_iota(jnp.int32, (b, size), 1), b - 1)

    def same_block(shift):
        return jnp.right_shift(prow, shift) == jnp.right_shift(pcol, shift)

    in_pack = jnp.right_shift(row, b.bit_length() - 1) == jnp.right_shift(col, b.bit_length() - 1)
    a_p = _each(lambda a: pack(jnp.where(in_pack, a, 0.0), b), a_list)
    eye_p = jnp.where(prow == pcol, 1.0, 0.0)
    inner = same_block(4)
    d_p = _each(lambda ap: jnp.where(inner, ap, 0.0), a_p)
    x_p = _each(lambda dp: eye_p - dp, d_p)
    p_bd = _each(lambda dp: unpack(dp, b), d_p)
    p_p = _each(_dot, d_p, p_bd)
    for step in range(3):
        p_bd = _each(lambda pp: unpack(pp, b), p_p)
        x_p = _each(lambda xp, pb: xp + _dot(xp, pb), x_p, p_bd)
        if step < 2:
            p_p = _each(_dot, p_p, p_bd)
    shift = 5
    while (1 << shift) <= b:
        outer = same_block(shift)
        coupling_mask = outer & jnp.logical_not(inner)
        x_bd = _each(lambda xp: unpack(xp, b), x_p)
        lx_p = _each(lambda ap, xb: _dot(jnp.where(coupling_mask, ap, 0.0), xb), a_p, x_bd)
        x_p = _each(lambda xp, lp: xp - _dot(xp, unpack(lp, b)), x_p, lx_p)
        inner = outer
        shift += 1
    x_bd = _each(lambda xp: unpack(xp, b), x_p)
    if chunk == b:
        return x_bd
    b2 = 2 * b
    in_pack2 = jnp.right_shift(row, b2.bit_length() - 1) == jnp.right_shift(col, b2.bit_length() - 1)
    x_p2 = _each(lambda xb: pack(xb, b2), x_bd)
    l_p2 = _each(lambda a: pack(jnp.where(in_pack2 & jnp.logical_not(in_pack), a, 0.0), b2), a_list)
    lx_p2 = _each(_dot, l_p2, x_bd)
    x_p2 = _each(lambda xp, lp: xp - _dot(xp, unpack(lp, b2)), x_p2, lx_p2)
    return _each(lambda xp: unpack(xp, b2), x_p2)


def _gdn_kernel(q_ref, k_ref, v_ref, zgate_ref, ba_ref, gnw_ref, o_ref,
                gate_scr, u_scr, lhs_scr, n_scr, egl_scr, intra_scr):
    seq = q_ref.shape[1]
    sup = GDN_SUPER
    chunk = GDN_CHUNK
    per_sup = sup // chunk
    nsup = seq // sup
    hps = GDN_HEADS_PER_STEP
    lhs_rows = 2 * chunk + GDN_HEAD_DIM
    group = pl.program_id(1)
    lane = lax.broadcasted_iota(jnp.int32, (1, LANES), 1)
    row = lax.broadcasted_iota(jnp.int32, (sup, sup), 0)
    col = lax.broadcasted_iota(jnp.int32, (sup, sup), 1)
    chunk_shift = chunk.bit_length() - 1
    same_chunk = jnp.right_shift(row, chunk_shift) == jnp.right_shift(col, chunk_shift)
    causal = same_chunk & (row >= col)
    strict = same_chunk & (row > col)

    @pl.when(group == 0)
    def _():
        tri = jnp.where(causal, 1.0, 0.0).astype(BF16)
        for s in range(nsup):
            rows = slice(s * sup, (s + 1) * sup)
            gates = ba_ref[0, rows, :]
            hi = gates.astype(BF16)
            rest = gates - hi.astype(F32)
            mid = rest.astype(BF16)
            lo = (rest - mid.astype(F32)).astype(BF16)
            gc = (jnp.dot(tri, hi, preferred_element_type=F32) + jnp.dot(tri, mid, preferred_element_type=F32)
                  + jnp.dot(tri, lo, preferred_element_type=F32))
            gate_scr[rows, :] = jnp.where(lane < GDN_HEADS, gates, gc)

    tiles_per_step = GDN_PREP_GROUP // hps

    def prepare(step, carry):
        loaded = []
        for t in range(tiles_per_step):
            rows = pl.ds(pl.multiple_of((step * tiles_per_step + t) * sup, sup), sup)
            for j in range(hps):
                cols = slice(j * LANES, (j + 1) * LANES)
                loaded.append((group * hps + j, gate_scr[rows, :], q_ref[0, rows, cols].astype(F32),
                               k_ref[0, rows, cols].astype(F32), v_ref[0, rows, cols].astype(F32)))
        results = prepare_tiles(*zip(*loaded))
        for t in range(tiles_per_step):
            for j in range(hps):
                store_tile(j, step * tiles_per_step + t, *[r[t * hps + j] for r in results])
        return carry

    def decay_terms(head, gates):
        beta = jnp.sum(jnp.where(lane == head, gates, 0.0), axis=1, keepdims=True)
        gc = jnp.sum(jnp.where(lane == GDN_HEADS + head, gates, 0.0), axis=1, keepdims=True)
        gc_b = jnp.broadcast_to(gc, (sup, LANES))
        gc_row = jnp.concatenate([gc_b[0:LANES].T, gc_b[LANES:].T], axis=1)[0:1, :]
        decay = jnp.exp(jnp.where(causal, gc - gc_row, NEG_INF))
        gl_b = jnp.concatenate(
            [jnp.broadcast_to(gc_b[(c + 1) * chunk - 1:(c + 1) * chunk, :], (chunk, LANES))
             for c in range(per_sup)], axis=0)
        return beta, decay, jnp.exp(gc), jnp.exp(gl_b - gc_b), jnp.exp(gl_b)

    def state_terms(k_dec, w, u, q_dec, egl):
        lhs, ns, egls = [], [], []
        for c in range(per_sup):
            cr = slice(c * chunk, (c + 1) * chunk)
            pn = lax.dot_general(k_dec[cr].astype(BF16),
                                 jnp.concatenate([w[cr], u[cr]], axis=1).astype(BF16),
                                 TN_DIMS, preferred_element_type=F32)
            lhs.append(jnp.concatenate([w[cr], q_dec[cr], pn[:, :GDN_HEAD_DIM]], axis=0).astype(BF16))
            ns.append(pn[:, GDN_HEAD_DIM:])
            egls.append(egl[c * chunk:c * chunk + 8])
        return lhs, ns, egls

    def prepare_tiles(heads, gates, qb, kb, vb):
        beta, decay, egc, e_tail, egl = zip(*_each(decay_terms, heads, gates))
        k_beta = _each(lambda k, b: k * b, kb, beta)
        a = _each(lambda kbt, k, dc: jnp.where(strict, _dot_nt(kbt, k) * dc, 0.0), k_beta, kb, decay)
        t_inv = _unit_lower_inverse(a, row, col, chunk)
        wu = _each(lambda t, kbt, e, v, b: _dot(t, jnp.concatenate([kbt * e, v * b], axis=1)),
                   t_inv, k_beta, egc, vb, beta)
        w = [x[:, :GDN_HEAD_DIM] for x in wu]
        u = [x[:, GDN_HEAD_DIM:] for x in wu]
        intra = _each(lambda q, k, dc: jnp.where(causal, _dot_nt(q, k) * dc, 0.0).astype(BF16),
                      qb, kb, decay)
        q_dec = _each(lambda q, e: q * e, qb, egc)
        k_dec = _each(lambda k, e: k * e, kb, e_tail)
        lhs, ns, egls = zip(*_each(state_terms, k_dec, w, u, q_dec, egl))
        return u, intra, lhs, ns, egls

    def chunk_slot(s, c, size):
        return pl.ds(pl.multiple_of((s * per_sup + c) * size, size), size)

    def store_tile(j, s, u, intra, lhs, ns, egls):
        rows = pl.ds(pl.multiple_of(s * sup, sup), sup)
        u_scr[j, rows, :] = u
        intra_scr[j, rows, :] = intra
        for c in range(per_sup):
            lhs_scr[j, chunk_slot(s, c, lhs_rows), :] = lhs[c]
            n_scr[j, chunk_slot(s, c, GDN_HEAD_DIM), :] = ns[c]
            egl_scr[j, chunk_slot(s, c, 8), :] = egls[c]

    lax.fori_loop(0, nsup // tiles_per_step, prepare, 0)

    gnw = gnw_ref[...]

    def scan(s, states):
        r0 = pl.multiple_of(s * sup, sup)
        states = list(states)
        v_parts = [[] for _ in range(hps)]
        for c in range(per_sup):
            rows = pl.ds(r0 + c * chunk, chunk)
            r = [jnp.dot(lhs_scr[j, chunk_slot(s, c, lhs_rows), :], states[j].astype(BF16),
                         preferred_element_type=F32) for j in range(hps)]
            outs = []
            for j in range(hps):
                v_parts[j].append(u_scr[j, rows, :] - r[j][:chunk])
                v_sup = jnp.concatenate(
                    v_parts[j] + [jnp.zeros(((per_sup - 1 - c) * chunk, LANES), F32)] * (c < per_sup - 1),
                    axis=0)
                o = r[j][chunk:2 * chunk] + jnp.dot(intra_scr[j, rows, :], v_sup.astype(BF16),
                                                    preferred_element_type=F32)
                egl = jnp.broadcast_to(egl_scr[j, chunk_slot(s, c, 8), :][0:1, :], (GDN_HEAD_DIM, LANES))
                states[j] = states[j] * egl - r[j][2 * chunk:] + n_scr[j, chunk_slot(s, c, GDN_HEAD_DIM), :]
                outs.append(o * lax.rsqrt(jnp.mean(o * o, axis=-1, keepdims=True) + RMS_EPS) * gnw)
            o_ref[0, rows, :] = (jnp.concatenate(outs, axis=1)
                                 * zgate_ref[0, rows, :].astype(F32)).astype(o_ref.dtype)
        return tuple(states)

    lax.fori_loop(0, nsup, scan, tuple(jnp.zeros((GDN_HEAD_DIM, GDN_HEAD_DIM), F32) for _ in range(hps)))


def _gdn(act_d, gates, gdn_norm_w, batch, seq):
    hps = GDN_HEADS_PER_STEP
    ngroup = GDN_HEADS // hps
    width = hps * GDN_HEAD_DIM
    assert GDN_PREP_GROUP % hps == 0 and (seq // GDN_SUPER) % (GDN_PREP_GROUP // hps) == 0
    act_d = act_d.reshape(batch, seq, 4 * GDN_WIDTH)
    gates = gates.reshape(batch, seq, LANES)

    def col_block(base):
        return pl.BlockSpec((1, seq, width), lambda b, g: (b, 0, base + g))

    def per_head(rows, lanes, dtype):
        return pltpu.VMEM((hps, rows, lanes), dtype)

    nchunk = seq // GDN_CHUNK
    scratch = [
        pltpu.VMEM((seq, LANES), F32),
        per_head(seq, LANES, F32),
        per_head(nchunk * (2 * GDN_CHUNK + GDN_HEAD_DIM), LANES, BF16),
        per_head(nchunk * GDN_HEAD_DIM, LANES, F32),
        per_head(nchunk * 8, LANES, F32),
        per_head(seq, GDN_SUPER, BF16),
    ]
    return pl.pallas_call(
        _gdn_kernel,
        grid=(batch, ngroup),
        in_specs=[col_block(0), col_block(ngroup), col_block(2 * ngroup), col_block(3 * ngroup),
                  pl.BlockSpec((1, seq, LANES), lambda b, g: (b, 0, 0)),
                  pl.BlockSpec((1, LANES), lambda b, g: (0, 0))],
        out_specs=pl.BlockSpec((1, seq, width), lambda b, g: (b, 0, g)),
        scratch_shapes=scratch,
        out_shape=jax.ShapeDtypeStruct((batch, seq, GDN_WIDTH), BF16),
        compiler_params=pltpu.CompilerParams(
            dimension_semantics=("arbitrary", "arbitrary"), vmem_limit_bytes=GDN_VMEM_LIMIT_BYTES),
        name="gdn",
    )(act_d, act_d, act_d, act_d, gates, gdn_norm_w)


def _outproj_kernel(x_ref, ma_ref, md_ref, wa_ref, wd_ref, nw_ref, o_ref):
    y = (jnp.dot(ma_ref[...], wa_ref[...], preferred_element_type=F32)
         + jnp.dot(md_ref[...], wd_ref[...], preferred_element_type=F32))
    y = y * lax.rsqrt(jnp.mean(y * y, axis=-1, keepdims=True) + RMS_EPS) * nw_ref[...]
    o_ref[...] = x_ref[...] + y


def _outproj(x2, mix_a, mix_d, w_a, w_d, norm_w, tm):
    n = x2.shape[0]
    return pl.pallas_call(
        _outproj_kernel,
        grid=(n // tm,),
        in_specs=[
            pl.BlockSpec((tm, D_MODEL), lambda i: (i, 0)),
            pl.BlockSpec((tm, MOBA_WIDTH), lambda i: (i, 0)),
            pl.BlockSpec((tm, GDN_WIDTH), lambda i: (i, 0)),
            pl.BlockSpec(w_a.shape, lambda i: (0, 0)),
            pl.BlockSpec(w_d.shape, lambda i: (0, 0)),
            pl.BlockSpec((1, D_MODEL), lambda i: (0, 0)),
        ],
        out_specs=pl.BlockSpec((tm, D_MODEL), lambda i: (i, 0)),
        out_shape=jax.ShapeDtypeStruct((n, D_MODEL), F32),
        compiler_params=pltpu.CompilerParams(
            dimension_semantics=("arbitrary",), vmem_limit_bytes=VMEM_LIMIT_BYTES),
        name="outproj",
    )(x2, mix_a, mix_d, w_a, w_d, norm_w)


def _alibi_slopes():
    start = 2.0 ** (-8.0 / MOBA_HEADS)
    slopes = np.asarray(start ** np.arange(1, MOBA_HEADS + 1), dtype=np.float32)
    return jnp.asarray(slopes)


def _layer(x, norm_pre_w, w_in, conv_w, a_log, dt_bias, gdn_norm_w, w_out, norm_post_w):
    batch, seq, _ = x.shape
    assert seq % MOBA_BLOCK == 0 and seq % GDN_SUPER == 0
    n = batch * seq
    tm = 512
    assert n % tm == 0
    x2 = x.reshape(n, D_MODEL)
    n_main = 4 * MOBA_WIDTH + 4 * GDN_WIDTH
    w_main = w_in[:, :n_main].astype(BF16)
    w_tail = jnp.pad(w_in[:, n_main:], ((0, 0), (0, LANES - 2 * GDN_HEADS))).astype(BF16)
    to_decay_lanes = (GDN_HEADS, LANES - 2 * GDN_HEADS)
    a_log_lanes = jnp.pad(a_log.astype(F32), to_decay_lanes).reshape(1, LANES)
    dt_bias_lanes = jnp.pad(dt_bias.astype(F32), to_decay_lanes).reshape(1, LANES)
    proj_a, act_d, gates = _inproj(x2, norm_pre_w.reshape(1, D_MODEL), w_main, w_tail, conv_w.astype(F32),
                                   a_log_lanes, dt_bias_lanes, tm, seq)
    mix_a = _moba(_alibi_slopes(), proj_a, batch, seq)
    mix_d = _gdn(act_d, gates, gdn_norm_w.reshape(1, GDN_HEAD_DIM), batch, seq)
    w_o = w_out.astype(BF16)
    out = _outproj(x2, mix_a.reshape(n, MOBA_WIDTH), mix_d.reshape(n, GDN_WIDTH),
                   w_o[:MOBA_WIDTH], w_o[MOBA_WIDTH:], norm_post_w.reshape(1, D_MODEL), 2 * tm)
    return out.reshape(batch, seq, D_MODEL)


def kernel(x, norm_pre_w, w_in, conv_w, a_log, dt_bias, gdn_norm_w, w_out, norm_post_w):
    for layer in range(norm_pre_w.shape[0]):
        x = _layer(x, norm_pre_w[layer], w_in[layer], conv_w[layer], a_log[layer], dt_bias[layer],
                   gdn_norm_w[layer], w_out[layer], norm_post_w[layer])
    return x
```

```python
import functools

import numpy as np
import jax
import jax.numpy as jnp
from jax import lax
from jax.experimental import pallas as pl
from jax.experimental.pallas import tpu as pltpu

F32 = jnp.float32
BF16 = jnp.bfloat16

D_MODEL = 1024
MOBA_WIDTH = 512
MOBA_HEAD_DIM = 64
MOBA_HEADS = MOBA_WIDTH // MOBA_HEAD_DIM
MOBA_BLOCK = 256
MOBA_TOPK = 3
GDN_WIDTH = 512
GDN_HEAD_DIM = 128
GDN_HEADS = GDN_WIDTH // GDN_HEAD_DIM
GDN_CONV = 4
GDN_CHUNK = 128
GDN_SUPER = 256
GDN_HEADS_PER_STEP = 4
NILPOTENT_SHIFT = 2
PACK_BLOCK = 64
GDN_PREP_GROUP = 8
RMS_EPS = 1e-6
NEG_INF = -1e30

INPROJ_GROUP = 512
LANES = 128
VMEM_LIMIT_BYTES = 48 * 1024 * 1024
GDN_VMEM_LIMIT_BYTES = 56 * 1024 * 1024

NT_DIMS = (((1,), (1,)), ((), ()))
TN_DIMS = (((0,), (0,)), ((), ()))


def _dot(a, b):
    return jnp.dot(a.astype(BF16), b.astype(BF16), preferred_element_type=F32)


def _dot_nt(a, b):
    return lax.dot_general(a.astype(BF16), b.astype(BF16), NT_DIMS, preferred_element_type=F32)


def _silu(x):
    return x / (1.0 + jnp.exp(-x))


def _l2_normalize(x):
    return x * lax.rsqrt(jnp.sum(x * x, axis=-1, keepdims=True) + RMS_EPS)


def _inproj_kernel(tiles_per_seq, x_ref, nw_ref, wm_ref, wt_ref, cw_ref, alog_ref, dtb_ref,
                   moba_ref, gdn_ref, gate_ref, pad_scr):
    tm = x_ref.shape[0]
    x = x_ref[...]
    h = x * lax.rsqrt(jnp.mean(x * x, axis=-1, keepdims=True) + RMS_EPS) * nw_ref[...]
    hb = h.astype(BF16)
    half = moba_ref.shape[-1]

    conv_width = pad_scr.shape[1]
    first = (pl.program_id(0) % tiles_per_seq) == 0

    @pl.when(first)
    def _():
        pad_scr[0:8, :] = jnp.zeros((8, conv_width), F32)

    @pl.when(jnp.logical_not(first))
    def _():
        pad_scr[0:8, :] = pad_scr[tm:tm + 8, :]

    gw = INPROJ_GROUP
    for g in range(half // gw):
        cols = slice(g * gw, (g + 1) * gw)
        d = jnp.dot(hb, wm_ref[:, half + g * gw:half + (g + 1) * gw], preferred_element_type=F32)
        moba_ref[:, cols] = jnp.dot(hb, wm_ref[:, cols], preferred_element_type=F32)
        conv_cols = slice(g * gw, min((g + 1) * gw, conv_width))
        if conv_cols.start < conv_cols.stop:
            pad_scr[8:, conv_cols] = d[:, :conv_cols.stop - conv_cols.start]
        for hh in range(gw // GDN_HEAD_DIM):
            in_cols = slice(g * gw + hh * GDN_HEAD_DIM, g * gw + (hh + 1) * GDN_HEAD_DIM)
            part = in_cols.start // GDN_WIDTH
            if part == 3:
                act = _silu(d[:, hh * GDN_HEAD_DIM:(hh + 1) * GDN_HEAD_DIM])
            else:
                acc = pad_scr[8:8 + tm, in_cols] * cw_ref[GDN_CONV - 1:GDN_CONV, in_cols]
                for j in range(GDN_CONV - 1):
                    shift = GDN_CONV - 1 - j
                    acc = acc + pad_scr[8 - shift:8 - shift + tm, in_cols] * cw_ref[j:j + 1, in_cols]
                act = _silu(acc)
            if part == 0:
                act = _l2_normalize(act) * (GDN_HEAD_DIM ** -0.5)
            elif part == 1:
                act = _l2_normalize(act)
            gdn_ref[:, in_cols] = act.astype(gdn_ref.dtype)
    ba = jnp.dot(hb, wt_ref[...], preferred_element_type=F32)

    lane = lax.broadcasted_iota(jnp.int32, (1, LANES), 1)
    pre = ba + dtb_ref[...]
    softplus = jnp.maximum(pre, 0.0) + jnp.log(1.0 + jnp.exp(-jnp.abs(pre)))
    gate_ref[...] = jnp.where(lane < GDN_HEADS, 1.0 / (1.0 + jnp.exp(-ba)),
                              -jnp.exp(alog_ref[...]) * softplus)


def _inproj(x2, norm_w, w_main, w_tail, conv_w, a_log_lanes, dt_bias_lanes, tm, seq):
    n = x2.shape[0]
    half = w_main.shape[1] // 2
    assert seq % tm == 0 and tm % 8 == 0
    return pl.pallas_call(
        functools.partial(_inproj_kernel, seq // tm),
        grid=(n // tm,),
        in_specs=[
            pl.BlockSpec((tm, D_MODEL), lambda i: (i, 0)),
            pl.BlockSpec((1, D_MODEL), lambda i: (0, 0)),
            pl.BlockSpec(w_main.shape, lambda i: (0, 0)),
            pl.BlockSpec(w_tail.shape, lambda i: (0, 0)),
            pl.BlockSpec(conv_w.shape, lambda i: (0, 0)),
            pl.BlockSpec((1, LANES), lambda i: (0, 0)),
            pl.BlockSpec((1, LANES), lambda i: (0, 0)),
        ],
        out_specs=[
            pl.BlockSpec((tm, half), lambda i: (i, 0)),
            pl.BlockSpec((tm, half), lambda i: (i, 0)),
            pl.BlockSpec((tm, LANES), lambda i: (i, 0)),
        ],
        out_shape=[
            jax.ShapeDtypeStruct((n, half), F32),
            jax.ShapeDtypeStruct((n, half), BF16),
            jax.ShapeDtypeStruct((n, LANES), F32),
        ],
        scratch_shapes=[pltpu.VMEM((tm + 8, conv_w.shape[1]), F32)],
        compiler_params=pltpu.CompilerParams(
            dimension_semantics=("arbitrary",), vmem_limit_bytes=VMEM_LIMIT_BYTES),
        name="inproj",
    )(x2, norm_w, w_main, w_tail, conv_w, a_log_lanes, dt_bias_lanes)


MOBA_MAX_BLOCKS = 8
FEAT_POS_HI = MOBA_MAX_BLOCKS
FEAT_POS_LO = MOBA_MAX_BLOCKS + 1
FEAT_ONE_HI = MOBA_MAX_BLOCKS + 2
FEAT_ONE_LO = MOBA_MAX_BLOCKS + 3


def _moba_features(seq):
    blk = MOBA_BLOCK
    pos = np.arange(seq)
    kfeat = np.zeros((2, seq, LANES), np.float32)
    qfeat = np.zeros((2, seq, LANES), np.float32)
    for h in range(2):
        base = MOBA_HEAD_DIM * (1 - h)
        kfeat[h, pos, base + pos // blk] = 1.0
        kfeat[h, :, base + FEAT_POS_HI] = blk * (pos // blk)
        kfeat[h, :, base + FEAT_POS_LO] = pos % blk
        kfeat[h, :, base + FEAT_ONE_HI] = 1.0
        kfeat[h, :, base + FEAT_ONE_LO] = 1.0
        qfeat[h, :, base + FEAT_POS_HI] = 1.0
        qfeat[h, :, base + FEAT_POS_LO] = 1.0
        qfeat[h, :, base + FEAT_ONE_HI] = -blk * (pos // blk)
        qfeat[h, :, base + FEAT_ONE_LO] = -(pos % blk)
    return jnp.asarray(kfeat), jnp.asarray(qfeat)


def _moba_kernel(slopes_ref, q_ref, k_ref, v_ref, z_ref, kfeat_ref, qfeat_ref, o_ref,
                 kaug_scr, qaug_scr, vaug_scr, logit_scr):
    seq = q_ref.shape[1]
    blk = MOBA_BLOCK
    nb = seq // blk
    pair = pl.program_id(1)
    lane = lax.broadcasted_iota(jnp.int32, (1, LANES), 1)
    head_masks = [(lane >= MOBA_HEAD_DIM * h) & (lane < MOBA_HEAD_DIM * (h + 1)) for h in range(2)]
    feat_lane = [lane - MOBA_HEAD_DIM * (1 - h) for h in range(2)]
    slopes = [slopes_ref[pair * 2 + h] for h in range(2)]

    k_all = k_ref[0]
    vaug_scr[:, :LANES] = v_ref[0].astype(BF16)
    vaug_scr[:, LANES:] = jnp.ones((seq, LANES), BF16)
    row8 = lax.broadcasted_iota(jnp.int32, (MOBA_MAX_BLOCKS, LANES), 0)
    kbar = jnp.zeros((MOBA_MAX_BLOCKS, LANES), F32)
    for n in range(nb):
        kbar = jnp.where(row8 == n, jnp.mean(k_all[n * blk:(n + 1) * blk], axis=0, keepdims=True), kbar)
    to_feat_lanes = [jnp.where(feat_lane[h] == row8, 1.0, 0.0) for h in range(2)]
    for h in range(2):
        kaug_scr[:, h * LANES:(h + 1) * LANES] = jnp.where(head_masks[h], k_all, kfeat_ref[h]).astype(BF16)
        qaug_scr[h, :, (1 - h) * LANES:(2 - h) * LANES] = jnp.zeros((seq, LANES), BF16)
    row = lax.broadcasted_iota(jnp.int32, (2 * blk, blk), 0)
    col = lax.broadcasted_iota(jnp.int32, (2 * blk, blk), 1)
    causal = jnp.bitwise_and(row, blk - 1) >= col

    q_all = q_ref[0]
    ranked = (MOBA_TOPK + 1) * blk
    block_id = lax.broadcasted_iota(jnp.int32, (MOBA_MAX_BLOCKS, seq - ranked), 0)
    own = jnp.right_shift(lax.broadcasted_iota(jnp.int32, (MOBA_MAX_BLOCKS, seq - ranked), 1) + ranked,
                          blk.bit_length() - 1)
    gates = [lax.dot_general(kbar, jnp.where(head_masks[h], q_all[ranked:], 0.0), NT_DIMS,
                             precision=lax.Precision.HIGHEST, preferred_element_type=F32)
             for h in range(2)]
    cnt = [jnp.zeros((MOBA_MAX_BLOCKS, seq - ranked), F32) for _ in range(2)]
    for other in range(nb - 1):
        for h in range(2):
            g_other = gates[h][other:other + 1, :]
            before = (g_other > gates[h]) | ((g_other == gates[h]) & (block_id > other))
            cnt[h] = cnt[h] + jnp.where(before & (own > other), 1.0, 0.0)
    for h in range(2):
        dropped = jnp.where((cnt[h] >= MOBA_TOPK) & (block_id < own), NEG_INF, 0.0)
        feat = slopes[h] * qfeat_ref[h]
        scaled = q_all * (MOBA_HEAD_DIM ** -0.5)
        own_lanes = slice(h * LANES, (h + 1) * LANES)
        qaug_scr[h, :ranked, own_lanes] = jnp.where(head_masks[h], scaled[:ranked], feat[:ranked]).astype(BF16)
        sel_feat = lax.dot_general(dropped, to_feat_lanes[h], TN_DIMS, preferred_element_type=F32)
        qaug_scr[h, ranked:, own_lanes] = jnp.where(head_masks[h], scaled[ranked:],
                                                    feat[ranked:] + sel_feat).astype(BF16)

    def logits_of(i):
        rows = slice(i * blk, (i + 1) * blk)
        q_aug = jnp.concatenate([qaug_scr[0, rows, :], qaug_scr[1, rows, :]], axis=0)
        mx = None
        for n in range(i + 1):
            tile = lax.dot_general(q_aug, kaug_scr[n * blk:(n + 1) * blk, :], NT_DIMS,
                                   preferred_element_type=F32)
            if n == i:
                tile = jnp.where(causal, tile, NEG_INF)
            logit_scr[i % 2, n] = tile
            halves = jnp.maximum(tile[:, :LANES], tile[:, LANES:])
            mx = halves if mx is None else jnp.maximum(mx, halves)
        return jnp.max(mx, axis=1, keepdims=True)

    m_next = logits_of(0)
    for i in range(nb):
        rows = slice(i * blk, (i + 1) * blk)
        m = m_next
        if i + 1 < nb:
            m_next = logits_of(i + 1)
        acc = [jnp.zeros((blk, 2 * LANES), F32) for _ in range(2)]
        for n in range(i + 1):
            p = jnp.exp(logit_scr[i % 2, n] - m).astype(BF16)
            acc = [acc[h] + jnp.dot(p[h * blk:(h + 1) * blk], vaug_scr[n * blk:(n + 1) * blk, :],
                                    preferred_element_type=F32) for h in range(2)]
        acc = jnp.concatenate(acc, axis=0)
        normed = acc[:, :LANES] / acc[:, LANES:]
        out = jnp.where(head_masks[0], normed[:blk], normed[blk:])
        o_ref[0, rows, :] = (out * _silu(z_ref[0, rows, :])).astype(o_ref.dtype)


def _moba(slopes, proj_a, batch, seq):
    npair = MOBA_WIDTH // LANES
    proj_a = proj_a.reshape(batch, seq, 4 * MOBA_WIDTH)
    nb = seq // MOBA_BLOCK
    assert nb <= MOBA_MAX_BLOCKS, "block-indicator feature lanes"
    kfeat, qfeat = _moba_features(seq)

    def col_block(base):
        return pl.BlockSpec((1, seq, LANES), lambda b, p, *_: (b, 0, base + p))

    return pl.pallas_call(
        _moba_kernel,
        grid_spec=pltpu.PrefetchScalarGridSpec(
            num_scalar_prefetch=1,
            grid=(batch, npair),
            in_specs=[col_block(0), col_block(npair), col_block(2 * npair), col_block(3 * npair),
                      pl.BlockSpec(kfeat.shape, lambda b, p, *_: (0, 0, 0)),
                      pl.BlockSpec(qfeat.shape, lambda b, p, *_: (0, 0, 0))],
            out_specs=pl.BlockSpec((1, seq, LANES), lambda b, p, *_: (b, 0, p)),
            scratch_shapes=[pltpu.VMEM((seq, 2 * LANES), BF16),
                            pltpu.VMEM((2, seq, 2 * LANES), BF16),
                            pltpu.VMEM((seq, 2 * LANES), BF16),
                            pltpu.VMEM((2, nb, 2 * MOBA_BLOCK, MOBA_BLOCK), F32)],
        ),
        out_shape=jax.ShapeDtypeStruct((batch, seq, MOBA_WIDTH), BF16),
        compiler_params=pltpu.CompilerParams(
            dimension_semantics=("arbitrary", "arbitrary"), vmem_limit_bytes=VMEM_LIMIT_BYTES),
        name="moba",
    )(slopes, proj_a, proj_a, proj_a, proj_a, kfeat, qfeat)


def _each(fn, *lists):
    return [fn(*args) for args in zip(*lists)]


def _unit_lower_inverse(a_list, row, col, chunk):
    size = a_list[0].shape[0]
    assert chunk in (PACK_BLOCK, 2 * PACK_BLOCK) and size % (2 * PACK_BLOCK) == 0

    def pack(m, b):
        out = m[0:b]
        for r in range(b, size, b):
            out = out + m[r:r + b]
        return out

    def unpack(pk, b):
        tiled = jnp.concatenate([pk] * (size // b), axis=0)
        return jnp.where(jnp.right_shift(row, b.bit_length() - 1) == jnp.right_shift(col, b.bit_length() - 1),
                         tiled, 0.0)

    b = PACK_BLOCK
    prow = lax.broadcasted_iota(jnp.int32, (b, size), 0)
    pcol = jnp.bitwise_and(lax.broadcasted_iota(jnp.int32, (b, size), 1), b - 1)

    def same_block(shift):
        return jnp.right_shift(prow, shift) == jnp.right_shift(pcol, shift)

    in_pack = jnp.right_shift(row, b.bit_length() - 1) == jnp.right_shift(col, b.bit_length() - 1)
    a_p = _each(lambda a: pack(jnp.where(in_pack, a, 0.0), b), a_list)
    eye_p = jnp.where(prow == pcol, 1.0, 0.0)
    inner = same_block(NILPOTENT_SHIFT)
    d_p = _each(lambda ap: jnp.where(inner, ap, 0.0), a_p)
    x_p = _each(lambda dp: eye_p - dp, d_p)
    p_bd = _each(lambda dp: unpack(dp, b), d_p)
    p_p = _each(_dot, d_p, p_bd)
    for step in range(NILPOTENT_SHIFT - 1):
        p_bd = _each(lambda pp: unpack(pp, b), p_p)
        x_p = _each(lambda xp, pb: xp + _dot(xp, pb), x_p, p_bd)
        if step < NILPOTENT_SHIFT - 2:
            p_p = _each(_dot, p_p, p_bd)
    shift = NILPOTENT_SHIFT + 1
    while (1 << shift) <= b:
        outer = same_block(shift)
        coupling_mask = outer & jnp.logical_not(inner)
        x_bd = _each(lambda xp: unpack(xp, b), x_p)
        lx_p = _each(lambda ap, xb: _dot(jnp.where(coupling_mask, ap, 0.0), xb), a_p, x_bd)
        x_p = _each(lambda xp, lp: xp - _dot(xp, unpack(lp, b)), x_p, lx_p)
        inner = outer
        shift += 1
    x_bd = _each(lambda xp: unpack(xp, b), x_p)
    if chunk == b:
        return x_bd
    b2 = 2 * b
    in_pack2 = jnp.right_shift(row, b2.bit_length() - 1) == jnp.right_shift(col, b2.bit_length() - 1)
    x_p2 = _each(lambda xb: pack(xb, b2), x_bd)
    l_p2 = _each(lambda a: pack(jnp.where(in_pack2 & jnp.logical_not(in_pack), a, 0.0), b2), a_list)
    lx_p2 = _each(_dot, l_p2, x_bd)
    x_p2 = _each(lambda xp, lp: xp - _dot(xp, unpack(lp, b2)), x_p2, lx_p2)
    return _each(lambda xp: unpack(xp, b2), x_p2)


def _gdn_kernel(q_ref, k_ref, v_ref, zgate_ref, ba_ref, gnw_ref, o_ref,
                gate_scr, u_scr, lhs_scr, n_scr, egl_scr, intra_scr):
    seq = q_ref.shape[1]
    sup = GDN_SUPER
    chunk = GDN_CHUNK
    per_sup = sup // chunk
    nsup = seq // sup
    hps = GDN_HEADS_PER_STEP
    lhs_rows = 2 * chunk + GDN_HEAD_DIM
    group = pl.program_id(1)
    lane = lax.broadcasted_iota(jnp.int32, (1, LANES), 1)
    row = lax.broadcasted_iota(jnp.int32, (sup, sup), 0)
    col = lax.broadcasted_iota(jnp.int32, (sup, sup), 1)
    chunk_shift = chunk.bit_length() - 1
    same_chunk = jnp.right_shift(row, chunk_shift) == jnp.right_shift(col, chunk_shift)
    causal = same_chunk & (row >= col)
    strict = same_chunk & (row > col)

    @pl.when(group == 0)
    def _():
        tri = jnp.where(causal, 1.0, 0.0).astype(BF16)
        for s in range(nsup):
            rows = slice(s * sup, (s + 1) * sup)
            gates = ba_ref[0, rows, :]
            hi = gates.astype(BF16)
            rest = gates - hi.astype(F32)
            mid = rest.astype(BF16)
            lo = (rest - mid.astype(F32)).astype(BF16)
            gc = (jnp.dot(tri, hi, preferred_element_type=F32) + jnp.dot(tri, mid, preferred_element_type=F32)
                  + jnp.dot(tri, lo, preferred_element_type=F32))
            gate_scr[rows, :] = jnp.where(lane < GDN_HEADS, gates, gc)

    tiles_per_step = GDN_PREP_GROUP // hps

    def prepare(step, carry):
        loaded = []
        for t in range(tiles_per_step):
            rows = pl.ds(pl.multiple_of((step * tiles_per_step + t) * sup, sup), sup)
            for j in range(hps):
                cols = slice(j * LANES, (j + 1) * LANES)
                loaded.append((group * hps + j, gate_scr[rows, :], q_ref[0, rows, cols].astype(F32),
                               k_ref[0, rows, cols].astype(F32), v_ref[0, rows, cols].astype(F32)))
        results = prepare_tiles(*zip(*loaded))
        for t in range(tiles_per_step):
            for j in range(hps):
                store_tile(j, step * tiles_per_step + t, *[r[t * hps + j] for r in results])
        return carry

    def decay_terms(head, gates):
        beta = jnp.sum(jnp.where(lane == head, gates, 0.0), axis=1, keepdims=True)
        gc = jnp.sum(jnp.where(lane == GDN_HEADS + head, gates, 0.0), axis=1, keepdims=True)
        gc_b = jnp.broadcast_to(gc, (sup, LANES))
        gc_row = jnp.concatenate([gc_b[0:LANES].T, gc_b[LANES:].T], axis=1)[0:1, :]
        decay = jnp.exp(jnp.where(causal, gc - gc_row, NEG_INF))
        gl_b = jnp.concatenate(
            [jnp.broadcast_to(gc_b[(c + 1) * chunk - 1:(c + 1) * chunk, :], (chunk, LANES))
             for c in range(per_sup)], axis=0)
        return beta, decay, jnp.exp(gc), jnp.exp(gl_b - gc_b), jnp.exp(gl_b)

    def state_terms(k_dec, w, u, q_dec, egl):
        lhs, ns, egls = [], [], []
        for c in range(per_sup):
            cr = slice(c * chunk, (c + 1) * chunk)
            pn = lax.dot_general(k_dec[cr].astype(BF16),
                                 jnp.concatenate([w[cr], u[cr]], axis=1).astype(BF16),
                                 TN_DIMS, preferred_element_type=F32)
            lhs.append(jnp.concatenate([w[cr], q_dec[cr], pn[:, :GDN_HEAD_DIM]], axis=0).astype(BF16))
            ns.append(pn[:, GDN_HEAD_DIM:])
            egls.append(egl[c * chunk:c * chunk + 8])
        return lhs, ns, egls

    def prepare_tiles(heads, gates, qb, kb, vb):
        beta, decay, egc, e_tail, egl = zip(*_each(decay_terms, heads, gates))
        k_beta = _each(lambda k, b: k * b, kb, beta)
        a = _each(lambda kbt, k, dc: jnp.where(strict, _dot_nt(kbt, k) * dc, 0.0), k_beta, kb, decay)
        t_inv = _unit_lower_inverse(a, row, col, chunk)
        wu = _each(lambda t, kbt, e, v, b: _dot(t, jnp.concatenate([kbt * e, v * b], axis=1)),
                   t_inv, k_beta, egc, vb, beta)
        w = [x[:, :GDN_HEAD_DIM] for x in wu]
        u = [x[:, GDN_HEAD_DIM:] for x in wu]
        intra = _each(lambda q, k, dc: jnp.where(causal, _dot_nt(q, k) * dc, 0.0).astype(BF16),
                      qb, kb, decay)
        q_dec = _each(lambda q, e: q * e, qb, egc)
        k_dec = _each(lambda k, e: k * e, kb, e_tail)
        lhs, ns, egls = zip(*_each(state_terms, k_dec, w, u, q_dec, egl))
        return u, intra, lhs, ns, egls

    def chunk_slot(s, c, size):
        return pl.ds(pl.multiple_of((s * per_sup + c) * size, size), size)

    def store_tile(j, s, u, intra, lhs, ns, egls):
        rows = pl.ds(pl.multiple_of(s * sup, sup), sup)
        u_scr[j, rows, :] = u
        intra_scr[j, rows, :] = intra
        for c in range(per_sup):
            lhs_scr[j, chunk_slot(s, c, lhs_rows), :] = lhs[c]
            n_scr[j, chunk_slot(s, c, GDN_HEAD_DIM), :] = ns[c]
            egl_scr[j, chunk_slot(s, c, 8), :] = egls[c]

    lax.fori_loop(0, nsup // tiles_per_step, prepare, 0)

    gnw = gnw_ref[...]

    def scan(s, states):
        r0 = pl.multiple_of(s * sup, sup)
        states = list(states)
        v_parts = [[] for _ in range(hps)]
        for c in range(per_sup):
            rows = pl.ds(r0 + c * chunk, chunk)
            r = [jnp.dot(lhs_scr[j, chunk_slot(s, c, lhs_rows), :], states[j].astype(BF16),
                         preferred_element_type=F32) for j in range(hps)]
            outs = []
            for j in range(hps):
                v_parts[j].append(u_scr[j, rows, :] - r[j][:chunk])
                v_sup = jnp.concatenate(
                    v_parts[j] + [jnp.zeros(((per_sup - 1 - c) * chunk, LANES), F32)] * (c < per_sup - 1),
                    axis=0)
                o = r[j][chunk:2 * chunk] + jnp.dot(intra_scr[j, rows, :], v_sup.astype(BF16),
                                                    preferred_element_type=F32)
                egl = jnp.broadcast_to(egl_scr[j, chunk_slot(s, c, 8), :][0:1, :], (GDN_HEAD_DIM, LANES))
                states[j] = states[j] * egl - r[j][2 * chunk:] + n_scr[j, chunk_slot(s, c, GDN_HEAD_DIM), :]
                outs.append(o * lax.rsqrt(jnp.mean(o * o, axis=-1, keepdims=True) + RMS_EPS) * gnw)
            o_ref[0, rows, :] = (jnp.concatenate(outs, axis=1)
                                 * zgate_ref[0, rows, :].astype(F32)).astype(o_ref.dtype)
        return tuple(states)

    lax.fori_loop(0, nsup, scan, tuple(jnp.zeros((GDN_HEAD_DIM, GDN_HEAD_DIM), F32) for _ in range(hps)))


def _gdn(act_d, gates, gdn_norm_w, batch, seq):
    hps = GDN_HEADS_PER_STEP
    ngroup = GDN_HEADS // hps
    width = hps * GDN_HEAD_DIM
    assert GDN_PREP_GROUP % hps == 0 and (seq // GDN_SUPER) % (GDN_PREP_GROUP // hps) == 0
    act_d = act_d.reshape(batch, seq, 4 * GDN_WIDTH)
    gates = gates.reshape(batch, seq, LANES)

    def col_block(base):
        return pl.BlockSpec((1, seq, width), lambda b, g: (b, 0, base + g))

    def per_head(rows, lanes, dtype):
        return pltpu.VMEM((hps, rows, lanes), dtype)

    nchunk = seq // GDN_CHUNK
    scratch = [
        pltpu.VMEM((seq, LANES), F32),
        per_head(seq, LANES, F32),
        per_head(nchunk * (2 * GDN_CHUNK + GDN_HEAD_DIM), LANES, BF16),
        per_head(nchunk * GDN_HEAD_DIM, LANES, F32),
        per_head(nchunk * 8, LANES, F32),
        per_head(seq, GDN_SUPER, BF16),
    ]
    return pl.pallas_call(
        _gdn_kernel,
        grid=(batch, ngroup),
        in_specs=[col_block(0), col_block(ngroup), col_block(2 * ngroup), col_block(3 * ngroup),
                  pl.BlockSpec((1, seq, LANES), lambda b, g: (b, 0, 0)),
                  pl.BlockSpec((1, LANES), lambda b, g: (0, 0))],
        out_specs=pl.BlockSpec((1, seq, width), lambda b, g: (b, 0, g)),
        scratch_shapes=scratch,
        out_shape=jax.ShapeDtypeStruct((batch, seq, GDN_WIDTH), BF16),
        compiler_params=pltpu.CompilerParams(
            dimension_semantics=("arbitrary", "arbitrary"), vmem_limit_bytes=GDN_VMEM_LIMIT_BYTES),
        name="gdn",
    )(act_d, act_d, act_d, act_d, gates, gdn_norm_w)


def _outproj_kernel(x_ref, ma_ref, md_ref, wa_ref, wd_ref, nw_ref, o_ref):
    y = (jnp.dot(ma_ref[...], wa_ref[...], preferred_element_type=F32)
         + jnp.dot(md_ref[...], wd_ref[...], preferred_element_type=F32))
    y = y * lax.rsqrt(jnp.mean(y * y, axis=-1, keepdims=True) + RMS_EPS) * nw_ref[...]
    o_ref[...] = x_ref[...] + y


def _outproj(x2, mix_a, mix_d, w_a, w_d, norm_w, tm):
    n = x2.shape[0]
    return pl.pallas_call(
        _outproj_kernel,
        grid=(n // tm,),
        in_specs=[
            pl.BlockSpec((tm, D_MODEL), lambda i: (i, 0)),
            pl.BlockSpec((tm, MOBA_WIDTH), lambda i: (i, 0)),
            pl.BlockSpec((tm, GDN_WIDTH), lambda i: (i, 0)),
            pl.BlockSpec(w_a.shape, lambda i: (0, 0)),
            pl.BlockSpec(w_d.shape, lambda i: (0, 0)),
            pl.BlockSpec((1, D_MODEL), lambda i: (0, 0)),
        ],
        out_specs=pl.BlockSpec((tm, D_MODEL), lambda i: (i, 0)),
        out_shape=jax.ShapeDtypeStruct((n, D_MODEL), F32),
        compiler_params=pltpu.CompilerParams(
            dimension_semantics=("arbitrary",), vmem_limit_bytes=VMEM_LIMIT_BYTES),
        name="outproj",
    )(x2, mix_a, mix_d, w_a, w_d, norm_w)


def _alibi_slopes():
    start = 2.0 ** (-8.0 / MOBA_HEADS)
    slopes = np.asarray(start ** np.arange(1, MOBA_HEADS + 1), dtype=np.float32)
    return jnp.asarray(slopes)


def _layer(x, norm_pre_w, w_in, conv_w, a_log, dt_bias, gdn_norm_w, w_out, norm_post_w):
    batch, seq, _ = x.shape
    assert seq % MOBA_BLOCK == 0 and seq % GDN_SUPER == 0
    n = batch * seq
    tm = 512
    assert n % tm == 0
    x2 = x.reshape(n, D_MODEL)
    n_main = 4 * MOBA_WIDTH + 4 * GDN_WIDTH
    w_main = w_in[:, :n_main].astype(BF16)
    w_tail = jnp.pad(w_in[:, n_main:], ((0, 0), (0, LANES - 2 * GDN_HEADS))).astype(BF16)
    to_decay_lanes = (GDN_HEADS, LANES - 2 * GDN_HEADS)
    a_log_lanes = jnp.pad(a_log.astype(F32), to_decay_lanes).reshape(1, LANES)
    dt_bias_lanes = jnp.pad(dt_bias.astype(F32), to_decay_lanes).reshape(1, LANES)
    proj_a, act_d, gates = _inproj(x2, norm_pre_w.reshape(1, D_MODEL), w_main, w_tail, conv_w.astype(F32),
                                   a_log_lanes, dt_bias_lanes, tm, seq)
    mix_a = _moba(_alibi_slopes(), proj_a, batch, seq)
    mix_d = _gdn(act_d, gates, gdn_norm_w.reshape(1, GDN_HEAD_DIM), batch, seq)
    w_o = w_out.astype(BF16)
    out = _outproj(x2, mix_a.reshape(n, MOBA_WIDTH), mix_d.reshape(n, GDN_WIDTH),
                   w_o[:MOBA_WIDTH], w_o[MOBA_WIDTH:], norm_post_w.reshape(1, D_MODEL), 2 * tm)
    return out.reshape(batch, seq, D_MODEL)


def kernel(x, norm_pre_w, w_in, conv_w, a_log, dt_bias, gdn_norm_w, w_out, norm_post_w):
    for layer in range(norm_pre_w.shape[0]):
        x = _layer(x, norm_pre_w[layer], w_in[layer], conv_w[layer], a_log[layer], dt_bias[layer],
                   gdn_norm_w[layer], w_out[layer], norm_post_w[layer])
    return x
```
